```python
import math
import jax, jax.numpy as jnp
from jax import lax
import numpy as np

D_MODEL = 1024
BATCH = 4
SEQ = 4096
DEPTH = 4

N_MIXERS = 2
HEAD_DIM = 64
N_ATTN_HEADS = D_MODEL // (2 * HEAD_DIM)
ATTN_QK_DIM = 2 * N_ATTN_HEADS * HEAD_DIM
ATTN_V_DIM = N_ATTN_HEADS * 2 * HEAD_DIM
NA_HEADS = D_MODEL // HEAD_DIM
GRID_W = 64
NA_KH = 8
NA_KW = 16
D_FF = 2816
CONV_W = 3
Q_BLOCK = 128
LN_EPS = 1e-5
RMS_EPS = 1e-5

kernel_name = "hybrid_diffattn_natten_convffn_deepnorm"


def layer_norm(x, g, b):
    xf = x.astype(jnp.float32)
    mu = jnp.mean(xf, axis=-1, keepdims=True)
    var = jnp.mean(jnp.square(xf - mu), axis=-1, keepdims=True)
    y = (xf - mu) * lax.rsqrt(var + LN_EPS) * g.astype(jnp.float32) + b.astype(jnp.float32)
    return y.astype(x.dtype)


def rms_norm(x, g):
    xf = x.astype(jnp.float32)
    y = xf * lax.rsqrt(jnp.mean(jnp.square(xf), axis=-1, keepdims=True) + RMS_EPS)
    return (y * g.astype(jnp.float32)).astype(x.dtype)


def alibi_slopes(n_heads):
    h = jnp.arange(1, n_heads + 1, dtype=jnp.float32)
    return jnp.exp2(-8.0 * h / n_heads)


def diff_attention(x, w_qkv, w_o, lam_vec, subln_g, lambda_init):
    B, S, _ = x.shape
    H, dh = N_ATTN_HEADS, HEAD_DIM
    qkv = x @ w_qkv
    q, k, v = jnp.split(qkv, [ATTN_QK_DIM, 2 * ATTN_QK_DIM], axis=-1)
    q = q.reshape(B, S, H, 2, dh) * (dh ** -0.5)
    k = k.reshape(B, S, H, 2, dh)
    v = v.reshape(B, S, H, 2 * dh)
    lf = lam_vec.astype(jnp.float32)
    lam = jnp.exp(jnp.sum(lf[0] * lf[1])) - jnp.exp(jnp.sum(lf[2] * lf[3])) + lambda_init
    slopes = alibi_slopes(H)
    pos = jnp.arange(S, dtype=jnp.float32)
    nb = S // Q_BLOCK
    q_blocks = jnp.transpose(q.reshape(B, nb, Q_BLOCK, H, 2, dh), (1, 0, 2, 3, 4, 5))
    q_pos = pos.reshape(nb, Q_BLOCK)

    def block(args):
        qb, qp = args
        s = jnp.einsum('bqhcd,bkhcd->bhcqk', qb, k).astype(jnp.float32)
        dist = jnp.abs(qp[:, None] - pos[None, :])
        s = s - slopes[None, :, None, None, None] * dist[None, None, None]
        p = jax.nn.softmax(s, axis=-1)
        a = p[:, :, 0] - lam * p[:, :, 1]
        return jnp.einsum('bhqk,bkhe->bqhe', a.astype(v.dtype), v)

    o = lax.map(block, (q_blocks, q_pos))
    o = jnp.transpose(o, (1, 0, 2, 3, 4)).reshape(B, S, H, 2 * dh)
    o = rms_norm(o, subln_g) * (1.0 - lambda_init)
    return o.reshape(B, S, H * 2 * dh) @ w_o


def neighborhood_attention(x, w_qkv, b_qkv, rpb, w_o, b_o):
    B, S, _ = x.shape
    rows = S // GRID_W
    kh = min(NA_KH, rows)
    kw = NA_KW
    H, dh = NA_HEADS, HEAD_DIM
    qkv = x @ w_qkv + b_qkv
    q, k, v = jnp.split(qkv, 3, axis=-1)
    q = q.reshape(B, rows, GRID_W, H, dh) * (dh ** -0.5)
    k = k.reshape(B, rows, GRID_W, H, dh)
    v = v.reshape(B, rows, GRID_W, H, dh)
    cols = jnp.arange(GRID_W, dtype=jnp.int32)
    col_start = jnp.clip(cols - kw // 2, 0, GRID_W - kw)
    col_idx = col_start[:, None] + jnp.arange(kw, dtype=jnp.int32)[None, :]
    col_off = col_idx - cols[:, None] + (NA_KW - 1)

    def row(r):
        r_start = jnp.clip(r - kh // 2, 0, rows - kh)
        k_rows = lax.dynamic_slice_in_dim(k, r_start, kh, axis=1)
        v_rows = lax.dynamic_slice_in_dim(v, r_start, kh, axis=1)
        k_sel = k_rows[:, :, col_idx]
        v_sel = v_rows[:, :, col_idx]
        q_row = lax.dynamic_index_in_dim(q, r, axis=1, keepdims=False)
        s = jnp.einsum('bchd,bicjhd->bhcij', q_row, k_sel).astype(jnp.float32)
        row_off = r_start + jnp.arange(kh, dtype=jnp.int32) - r + (NA_KH - 1)
        bias = rpb[:, row_off[None, :, None], col_off[:, None, :]]
        s = s + bias[None].astype(jnp.float32)
        p = jax.nn.softmax(s.reshape(B, H, GRID_W, kh * kw), axis=-1)
        p = p.reshape(B, H, GRID_W, kh, kw)
        return jnp.einsum('bhcij,bicjhd->bchd', p.astype(v.dtype), v_sel)

    o = lax.map(row, jnp.arange(rows, dtype=jnp.int32))
    o = jnp.transpose(o, (1, 0, 2, 3, 4)).reshape(B, S, H * dh)
    return o @ w_o + b_o


def conv_ffn(x, w_in, b_in, conv_w, conv_b, w_out, b_out):
    h = x @ w_in + b_in
    hp = jnp.pad(h, ((0, 0), (1, 1), (0, 0)))
    h = hp[:, :-2] * conv_w[0] + hp[:, 1:-1] * conv_w[1] + hp[:, 2:] * conv_w[2] + conv_b
    u, g = jnp.split(h, 2, axis=-1)
    return (u * jax.nn.gelu(g, approximate=False)) @ w_out + b_out


def setup_inputs(seed: int = 0) -> dict:
    key = jax.random.key(seed)
    ks = jax.random.split(key, 20)
    n_a = (DEPTH + 1) // 2
    n_b = DEPTH // 2
    beta = (8.0 * DEPTH) ** -0.25
    sd = D_MODEL ** -0.5
    nrm = jax.random.normal
    x = nrm(ks[0], (BATCH, SEQ, D_MODEL), jnp.float32)
    attn_w_qk = nrm(ks[1], (n_a, D_MODEL, 2 * ATTN_QK_DIM), jnp.float32) * sd
    attn_w_v = nrm(ks[2], (n_a, D_MODEL, ATTN_V_DIM), jnp.float32) * (sd * beta)
    attn_w_qkv = jnp.concatenate([attn_w_qk, attn_w_v], axis=-1)
    attn_w_o = nrm(ks[3], (n_a, ATTN_V_DIM, D_MODEL), jnp.float32) * (ATTN_V_DIM ** -0.5 * beta)
    attn_lambda = nrm(ks[4], (n_a, 4, HEAD_DIM), jnp.float32) * 0.1
    attn_subln_g = 1.0 + 0.02 * nrm(ks[5], (n_a, 2 * HEAD_DIM), jnp.float32)
    na_dim = NA_HEADS * HEAD_DIM
    na_w_qk = nrm(ks[6], (n_b, D_MODEL, 2 * na_dim), jnp.float32) * sd
    na_w_v = nrm(ks[7], (n_b, D_MODEL, na_dim), jnp.float32) * (sd * beta)
    na_w_qkv = jnp.concatenate([na_w_qk, na_w_v], axis=-1)
    na_b_qkv = 0.01 * nrm(ks[8], (n_b, 3 * na_dim), jnp.float32)
    na_rpb = 0.05 * nrm(ks[9], (n_b, NA_HEADS, 2 * NA_KH - 1, 2 * NA_KW - 1), jnp.float32)
    na_w_o = nrm(ks[10], (n_b, na_dim, D_MODEL), jnp.float32) * (na_dim ** -0.5 * beta)
    na_b_o = 0.01 * nrm(ks[11], (n_b, D_MODEL), jnp.float32)
    ffn_w_in = nrm(ks[12], (DEPTH, D_MODEL, 2 * D_FF), jnp.float32) * (sd * beta)
    ffn_b_in = 0.01 * nrm(ks[13], (DEPTH, 2 * D_FF), jnp.float32)
    ffn_conv_w = nrm(ks[14], (DEPTH, CONV_W, 2 * D_FF), jnp.float32) * (CONV_W ** -0.5)
    ffn_conv_b = 0.01 * nrm(ks[15], (DEPTH, 2 * D_FF), jnp.float32)
    ffn_w_out = nrm(ks[16], (DEPTH, D_FF, D_MODEL), jnp.float32) * (D_FF ** -0.5 * beta)
    ffn_b_out = 0.01 * nrm(ks[17], (DEPTH, D_MODEL), jnp.float32)
    ln_g = 1.0 + 0.02 * nrm(ks[18], (DEPTH, 2, D_MODEL), jnp.float32)
    ln_b = 0.02 * nrm(ks[19], (DEPTH, 2, D_MODEL), jnp.float32)
    return {"x": x, "attn_w_qkv": attn_w_qkv, "attn_w_o": attn_w_o,
            "attn_lambda": attn_lambda, "attn_subln_g": attn_subln_g,
            "na_w_qkv": na_w_qkv, "na_b_qkv": na_b_qkv, "na_rpb": na_rpb,
            "na_w_o": na_w_o, "na_b_o": na_b_o,
            "ffn_w_in": ffn_w_in, "ffn_b_in": ffn_b_in, "ffn_conv_w": ffn_conv_w,
            "ffn_conv_b": ffn_conv_b, "ffn_w_out": ffn_w_out, "ffn_b_out": ffn_b_out,
            "ln_g": ln_g, "ln_b": ln_b}


def reference(x, attn_w_qkv, attn_w_o, attn_lambda, attn_subln_g,
              na_w_qkv, na_b_qkv, na_rpb, na_w_o, na_b_o,
              ffn_w_in, ffn_b_in, ffn_conv_w, ffn_conv_b, ffn_w_out, ffn_b_out,
              ln_g, ln_b):
    alpha = (2.0 * DEPTH) ** 0.25
    for i in range(DEPTH):
        j = i // N_MIXERS
        if i % N_MIXERS == 0:
            lambda_init = 0.8 - 0.6 * math.exp(-0.3 * i)
            mix = diff_attention(x, attn_w_qkv[j], attn_w_o[j], attn_lambda[j],
                                 attn_subln_g[j], lambda_init)
        else:
            mix = neighborhood_attention(x, na_w_qkv[j], na_b_qkv[j], na_rpb[j],
                                         na_w_o[j], na_b_o[j])
        x = layer_norm(alpha * x + mix, ln_g[i, 0], ln_b[i, 0])
        ff = conv_ffn(x, ffn_w_in[i], ffn_b_in[i], ffn_conv_w[i], ffn_conv_b[i],
                      ffn_w_out[i], ffn_b_out[i])
        x = layer_norm(alpha * x + ff, ln_g[i, 1], ln_b[i, 1])
    return x
```

```python
import functools
import math

import jax
import jax.numpy as jnp
from jax import lax
from jax.experimental import pallas as pl
from jax.experimental.pallas import tpu as pltpu

DEPTH = 4
HEAD_DIM = 64
GRID_W = 64
NA_KH = 8
NA_KW = 16
LN_EPS = 1e-5
RMS_EPS = 1e-5
ALPHA = (2.0 * DEPTH) ** 0.25

LANES = 128
VMEM_LIMIT_BYTES = 56 * 1024 * 1024

NEG_BIG = -1e30

PROJ_TM = 1024
PROJ_TN = 1024
OUT_TM = 512
FFN_TM = 1024
FFN_TN = 256
ATT_TQ = 256
ATT_TK = 512
NA_QROWS = 4
NA_KROWS = NA_QROWS + NA_KH


def _cparams(sem):
    return pltpu.CompilerParams(dimension_semantics=sem, vmem_limit_bytes=VMEM_LIMIT_BYTES)


def _layer_norm_rows(z, g, b):
    mu = jnp.mean(z, axis=-1, keepdims=True)
    zc = z - mu
    var = jnp.mean(zc * zc, axis=-1, keepdims=True)
    return zc * lax.rsqrt(var + LN_EPS) * g + b


def _proj_kernel(x_ref, w_ref, b_ref, s_ref, o_ref):
    acc = jnp.dot(x_ref[...].astype(jnp.bfloat16), w_ref[...], preferred_element_type=jnp.float32)
    o_ref[...] = ((acc + b_ref[...]) * s_ref[...]).astype(o_ref.dtype)


def _qkv_proj(x, w, bias, colscale):
    m, d = x.shape
    n = w.shape[1]
    return pl.pallas_call(
        _proj_kernel,
        grid=(m // PROJ_TM, n // PROJ_TN),
        in_specs=[
            pl.BlockSpec((PROJ_TM, d), lambda i, j: (i, 0)),
            pl.BlockSpec((d, PROJ_TN), lambda i, j: (0, j)),
            pl.BlockSpec((1, PROJ_TN), lambda i, j: (0, j)),
            pl.BlockSpec((1, PROJ_TN), lambda i, j: (0, j)),
        ],
        out_specs=pl.BlockSpec((PROJ_TM, PROJ_TN), lambda i, j: (i, j)),
        out_shape=jax.ShapeDtypeStruct((m, n), jnp.bfloat16),
        compiler_params=_cparams(("parallel", "arbitrary")),
        name="qkv_proj",
    )(x, w, bias, colscale)


def _out_ln_kernel(o_ref, x_ref, w_ref, b_ref, g_ref, beta_ref, y_ref):
    mix = jnp.dot(o_ref[...], w_ref[...], preferred_element_type=jnp.float32) + b_ref[...]
    z = ALPHA * x_ref[...] + mix
    y_ref[...] = _layer_norm_rows(z, g_ref[...], beta_ref[...])


def _out_proj_ln(o, x, w, b, g, beta):
    m, d = x.shape
    k = o.shape[1]
    row = lambda i: (i, 0)
    fixed = lambda i: (0, 0)
    return pl.pallas_call(
        _out_ln_kernel,
        grid=(m // OUT_TM,),
        in_specs=[
            pl.BlockSpec((OUT_TM, k), row),
            pl.BlockSpec((OUT_TM, d), row),
            pl.BlockSpec((k, d), fixed),
            pl.BlockSpec((1, d), fixed),
            pl.BlockSpec((1, d), fixed),
            pl.BlockSpec((1, d), fixed),
        ],
        out_specs=pl.BlockSpec((OUT_TM, d), row),
        out_shape=jax.ShapeDtypeStruct((m, d), jnp.float32),
        compiler_params=_cparams(("parallel",)),
        name="out_proj_ln",
    )(o, x, w, b, g, beta)


def _diff_attn_kernel(slopes_ref, q_ref, k_ref, v_ref, lam_ref, g_ref, o_ref,
                      s_scr, acc_scr, *, lambda_init, n_chunks):
    h = pl.program_id(1)
    qi = pl.program_id(2)
    tq = q_ref.shape[0]
    slope = slopes_ref[h]

    q = q_ref[...]
    lane = lax.broadcasted_iota(jnp.int32, q.shape, 1)
    zero = jnp.zeros_like(q)
    q_parts = (jnp.where(lane < HEAD_DIM, q, zero), jnp.where(lane >= HEAD_DIM, q, zero))

    row = lax.broadcasted_iota(jnp.int32, (tq, ATT_TK), 0)
    col = lax.broadcasted_iota(jnp.int32, (tq, ATT_TK), 1)
    rel = (row - col + qi * tq).astype(jnp.float32)

    def score_chunk(j, carry):
        m1, m2 = carry
        start = pl.multiple_of(j * ATT_TK, ATT_TK)
        kc = k_ref[pl.ds(start, ATT_TK), :]
        dist = jnp.abs(rel - (j * ATT_TK).astype(jnp.float32))
        pen = slope * dist
        out = []
        for c, mc in ((0, m1), (1, m2)):
            s = lax.dot_general(q_parts[c], kc, (((1,), (1,)), ((), ())),
                                preferred_element_type=jnp.float32) - pen
            s_scr[c, j] = s
            out.append(jnp.maximum(mc, jnp.max(s, axis=-1, keepdims=True)))
        return tuple(out)

    neg = jnp.full((tq, 1), -jnp.inf, jnp.float32)
    m1, m2 = lax.fori_loop(0, n_chunks, score_chunk, (neg, neg))

    acc_scr[...] = jnp.zeros_like(acc_scr)

    def pv_chunk(j, carry):
        l1, l2 = carry
        start = pl.multiple_of(j * ATT_TK, ATT_TK)
        vc = v_ref[pl.ds(start, ATT_TK), :]
        out = []
        for c, mc, lc in ((0, m1, l1), (1, m2, l2)):
            p = jnp.exp(s_scr[c, j] - mc)
            out.append(lc + jnp.sum(p, axis=-1, keepdims=True))
            acc_scr[c] += jnp.dot(p.astype(jnp.bfloat16), vc, preferred_element_type=jnp.float32)
        return tuple(out)

    zl = jnp.zeros((tq, 1), jnp.float32)
    l1, l2 = lax.fori_loop(0, n_chunks, pv_chunk, (zl, zl))

    lf = lam_ref[...]
    lam = (jnp.exp(jnp.sum(lf[0:1, :] * lf[1:2, :], axis=-1, keepdims=True))
           - jnp.exp(jnp.sum(lf[2:3, :] * lf[3:4, :], axis=-1, keepdims=True)) + lambda_init)
    o = acc_scr[0] / l1 - lam * (acc_scr[1] / l2)
    y = o * lax.rsqrt(jnp.mean(o * o, axis=-1, keepdims=True) + RMS_EPS) * g_ref[...]
    o_ref[...] = (y * (1.0 - lambda_init)).astype(o_ref.dtype)


def _diff_attention(qkv, slopes, lam_vec, subln_g, batch, seq, lambda_init):
    n_heads = qkv.shape[1] // (3 * LANES)
    nq = seq // ATT_TQ
    n_chunks = seq // ATT_TK
    kern = functools.partial(_diff_attn_kernel, lambda_init=lambda_init, n_chunks=n_chunks)
    return pl.pallas_call(
        kern,
        grid=(batch, n_heads, nq),
        in_specs=[
            pl.BlockSpec(memory_space=pltpu.SMEM),
            pl.BlockSpec((ATT_TQ, LANES), lambda b, h, i: (b * nq + i, h)),
            pl.BlockSpec((seq, LANES), lambda b, h, i: (b, n_heads + h)),
            pl.BlockSpec((seq, LANES), lambda b, h, i: (b, 2 * n_heads + h)),
            pl.BlockSpec((4, HEAD_DIM), lambda b, h, i: (0, 0)),
            pl.BlockSpec((1, LANES), lambda b, h, i: (0, 0)),
        ],
        out_specs=pl.BlockSpec((ATT_TQ, LANES), lambda b, h, i: (b * nq + i, h)),
        out_shape=jax.ShapeDtypeStruct((batch * seq, n_heads * LANES), jnp.bfloat16),
        scratch_shapes=[
            pltpu.VMEM((2, n_chunks, ATT_TQ, ATT_TK), jnp.float32),
            pltpu.VMEM((2, ATT_TQ, LANES), jnp.float32),
        ],
        compiler_params=_cparams(("parallel", "parallel", "arbitrary")),
        name="diff_attn",
    )(slopes, qkv, qkv, qkv, lam_vec, subln_g)


def _na_row_offset(cls, qr, kr):
    if cls == 0:
        return kr - qr + NA_KH - 1 if kr < NA_KH else None
    if cls == 2:
        return kr - qr - 1 if kr >= NA_KROWS - NA_KH else None
    return kr - qr + NA_KH // 2 - 1 if 0 <= kr - qr < NA_KH else None


def _na_build_bias(rpb_ref, bias_scr):
    shape = (GRID_W, LANES)
    qc = lax.broadcasted_iota(jnp.int32, shape, 0)
    lane = lax.broadcasted_iota(jnp.int32, shape, 1)
    kc = lane % GRID_W
    col_start = jnp.clip(qc - NA_KW // 2, 0, GRID_W - NA_KW)
    in_cols = (kc >= col_start) & (kc < col_start + NA_KW)
    left = lane < GRID_W
    neg = jnp.full(shape, NEG_BIG, jnp.float32)

    def toeplitz(hh, ro, lane_off):
        vec = jnp.broadcast_to(rpb_ref[hh, ro:ro + 1, :], shape)
        shift = (lane_off - (NA_KW - 1)) % LANES
        return pltpu.roll(vec, shift, 1, stride=1, stride_axis=0)

    for cls in range(3):
        for hh in range(2):
            for qr in range(NA_QROWS):
                for kp in range(NA_KROWS // 2):
                    ro_a = _na_row_offset(cls, qr, 2 * kp)
                    ro_b = _na_row_offset(cls, qr, 2 * kp + 1)
                    blk_a = neg if ro_a is None else jnp.where(in_cols, toeplitz(hh, ro_a, 0), neg)
                    blk_b = neg if ro_b is None else jnp.where(in_cols, toeplitz(hh, ro_b, GRID_W), neg)
                    bias_scr[cls, hh, qr * GRID_W:(qr + 1) * GRID_W, kp * LANES:(kp + 1) * LANES] = (
                        jnp.where(left, blk_a, blk_b))


def _na_kernel(q_ref, k_ref, v_ref, rpb_ref, o_ref, bias_scr, *, n_tiles):
    b = pl.program_id(1)
    t = pl.program_id(2)

    @pl.when((b == 0) & (t == 0))
    def _():
        _na_build_bias(rpb_ref, bias_scr)

    n_keys = NA_KROWS * GRID_W
    rows_total = n_tiles * NA_QROWS
    ws = jnp.clip(t * NA_QROWS - NA_KH // 2, 0, rows_total - NA_KROWS)
    start = pl.multiple_of(ws * GRID_W, GRID_W)
    kw = k_ref[pl.ds(start, n_keys), :]
    vw = v_ref[pl.ds(start, n_keys), :]
    cls = jnp.where(t == 0, 0, jnp.where(t == n_tiles - 1, 2, 1))

    q = q_ref[...]
    lane = lax.broadcasted_iota(jnp.int32, q.shape, 1)
    zero = jnp.zeros_like(q)
    outs = []
    for hh in range(2):
        qm = jnp.where((lane >= hh * HEAD_DIM) & (lane < (hh + 1) * HEAD_DIM), q, zero)
        s = lax.dot_general(qm, kw, (((1,), (1,)), ((), ())), preferred_element_type=jnp.float32)
        s = s + bias_scr[cls, hh]
        m = jnp.max(s, axis=-1, keepdims=True)
        p = jnp.exp(s - m)
        l = jnp.sum(p, axis=-1, keepdims=True)
        outs.append(jnp.dot(p.astype(jnp.bfloat16), vw, preferred_element_type=jnp.float32) / l)
    o_ref[...] = jnp.where(lane < HEAD_DIM, outs[0], outs[1]).astype(o_ref.dtype)


def _neighborhood_attention(qkv, rpb_pad, batch, seq):
    n_pairs = qkv.shape[1] // (3 * LANES)
    tq = NA_QROWS * GRID_W
    n_tiles = seq // tq
    assert seq // GRID_W >= NA_KROWS and seq % tq == 0
    kern = functools.partial(_na_kernel, n_tiles=n_tiles)
    return pl.pallas_call(
        kern,
        grid=(n_pairs, batch, n_tiles),
        in_specs=[
            pl.BlockSpec((tq, LANES), lambda hp, b, t: (b * n_tiles + t, hp)),
            pl.BlockSpec((seq, LANES), lambda hp, b, t: (b, n_pairs + hp)),
            pl.BlockSpec((seq, LANES), lambda hp, b, t: (b, 2 * n_pairs + hp)),
            pl.BlockSpec((2, 2 * NA_KH - 1, LANES), lambda hp, b, t: (hp, 0, 0)),
        ],
        out_specs=pl.BlockSpec((tq, LANES), lambda hp, b, t: (b * n_tiles + t, hp)),
        out_shape=jax.ShapeDtypeStruct((batch * seq, n_pairs * LANES), jnp.bfloat16),
        scratch_shapes=[pltpu.VMEM((3, 2, tq, NA_KROWS * GRID_W), jnp.float32)],
        compiler_params=_cparams(("arbitrary", "arbitrary", "arbitrary")),
        name="na_attn",
    )(qkv, qkv, qkv, rpb_pad)


def _ffn_kernel(x_ref, xp_ref, xn_ref, wu_ref, wg_ref, bu_ref, bg_ref, cwu_ref, cwg_ref,
                cbu_ref, cbg_ref, wo_ref, bo_ref, g_ref, beta_ref, y_ref, xb_scr, acc_scr,
                *, tiles_per_seq):
    i = pl.program_id(0)
    j = pl.program_id(1)
    tm = x_ref.shape[0]

    @pl.when(j == 0)
    def _():
        xb_scr[...] = x_ref[...].astype(jnp.bfloat16)
        acc_scr[...] = jnp.zeros_like(acc_scr)

    xb = xb_scr[...]
    has_prev = (i % tiles_per_seq != 0).astype(jnp.float32)
    has_next = (i % tiles_per_seq != tiles_per_seq - 1).astype(jnp.float32)
    xp = xp_ref[...].astype(jnp.bfloat16)
    xn = xn_ref[...].astype(jnp.bfloat16)
    row = lax.broadcasted_iota(jnp.int32, (tm, wu_ref.shape[1]), 0)
    first = row == 0
    last = row == tm - 1

    def conv_half(w_ref, b_ref, cw_ref, cb_ref):
        w = w_ref[...]
        bias = b_ref[...]
        h = jnp.dot(xb, w, preferred_element_type=jnp.float32) + bias
        hp = (jnp.dot(xp, w, preferred_element_type=jnp.float32)[7:8, :] + bias) * has_prev
        hn = (jnp.dot(xn, w, preferred_element_type=jnp.float32)[0:1, :] + bias) * has_next
        h_prev = jnp.where(first, hp, pltpu.roll(h, 1, 0))
        h_next = jnp.where(last, hn, pltpu.roll(h, tm - 1, 0))
        cw = cw_ref[...]
        return h_prev * cw[0:1, :] + h * cw[1:2, :] + h_next * cw[2:3, :] + cb_ref[...]

    u = conv_half(wu_ref, bu_ref, cwu_ref, cbu_ref)
    g = conv_half(wg_ref, bg_ref, cwg_ref, cbg_ref)
    gelu = 0.5 * g * (1.0 + lax.erf(g * (1.0 / math.sqrt(2.0))))
    a = (u * gelu).astype(jnp.bfloat16)
    acc_scr[...] += jnp.dot(a, wo_ref[...], preferred_element_type=jnp.float32)

    @pl.when(j == pl.num_programs(1) - 1)
    def _():
        z = ALPHA * x_ref[...] + acc_scr[...] + bo_ref[...]
        y_ref[...] = _layer_norm_rows(z, g_ref[...], beta_ref[...])


def _conv_ffn_ln(x, seq, w_in, b_in, conv_w, conv_b, w_out, b_out, g, beta):
    m, d = x.shape
    d_ff = w_out.shape[0]
    nj = d_ff // FFN_TN
    n_i = m // FFN_TM
    tiles_per_seq = seq // FFN_TM
    rows8 = FFN_TM // 8
    n_blk8 = m // 8
    u_col = lambda i, j: (0, j)
    g_col = lambda i, j: (0, nj + j)
    fixed = lambda i, j: (0, 0)
    kern = functools.partial(_ffn_kernel, tiles_per_seq=tiles_per_seq)
    return pl.pallas_call(
        kern,
        grid=(n_i, nj),
        in_specs=[
            pl.BlockSpec((FFN_TM, d), lambda i, j: (i, 0)),
            pl.BlockSpec((8, d), lambda i, j: (jnp.maximum(i * rows8 - 1, 0), 0)),
            pl.BlockSpec((8, d), lambda i, j: (jnp.minimum((i + 1) * rows8, n_blk8 - 1), 0)),
            pl.BlockSpec((d, FFN_TN), u_col),
            pl.BlockSpec((d, FFN_TN), g_col),
            pl.BlockSpec((1, FFN_TN), u_col),
            pl.BlockSpec((1, FFN_TN), g_col),
            pl.BlockSpec((3, FFN_TN), u_col),
            pl.BlockSpec((3, FFN_TN), g_col),
            pl.BlockSpec((1, FFN_TN), u_col),
            pl.BlockSpec((1, FFN_TN), g_col),
            pl.BlockSpec((FFN_TN, d), lambda i, j: (j, 0)),
            pl.BlockSpec((1, d), fixed),
            pl.BlockSpec((1, d), fixed),
            pl.BlockSpec((1, d), fixed),
        ],
        out_specs=pl.BlockSpec((FFN_TM, d), lambda i, j: (i, 0)),
        out_shape=jax.ShapeDtypeStruct((m, d), jnp.float32),
        scratch_shapes=[
            pltpu.VMEM((FFN_TM, d), jnp.bfloat16),
            pltpu.VMEM((FFN_TM, d), jnp.float32),
        ],
        compiler_params=_cparams(("parallel", "arbitrary")),
        name="conv_ffn_ln",
    )(x, x, x, w_in, w_in, b_in, b_in, conv_w, conv_w, conv_b, conv_b, w_out, b_out, g, beta)


def kernel(x, attn_w_qkv, attn_w_o, attn_lambda, attn_subln_g, na_w_qkv, na_b_qkv, na_rpb,
           na_w_o, na_b_o, ffn_w_in, ffn_b_in, ffn_conv_w, ffn_conv_b, ffn_w_out, ffn_b_out,
           ln_g, ln_b):
    batch, seq, d = x.shape
    bf16 = jnp.bfloat16
    f32 = jnp.float32
    xf = x.reshape(batch * seq, d)
    q_scale = HEAD_DIM ** -0.5

    n_attn_heads = attn_w_qkv.shape[2] // (3 * 2 * HEAD_DIM)
    slopes = jnp.exp2(-8.0 * jnp.arange(1, n_attn_heads + 1, dtype=f32) / n_attn_heads)
    qk_cols = 2 * n_attn_heads * HEAD_DIM
    attn_scale = jnp.where(jnp.arange(attn_w_qkv.shape[2]) < qk_cols, q_scale, 1.0).astype(f32)[None]
    na_dim = na_w_qkv.shape[2] // 3
    na_scale = jnp.where(jnp.arange(3 * na_dim) < na_dim, q_scale, 1.0).astype(f32)[None]
    zeros_d = jnp.zeros((1, d), f32)

    for i in range(DEPTH):
        j = i // 2
        if i % 2 == 0:
            lambda_init = 0.8 - 0.6 * math.exp(-0.3 * i)
            w = attn_w_qkv[j].astype(bf16)
            qkv = _qkv_proj(xf, w, jnp.zeros((1, w.shape[1]), f32), attn_scale)
            o = _diff_attention(qkv, slopes, attn_lambda[j], attn_subln_g[j][None], batch, seq,
                                lambda_init)
            w_o, b_o = attn_w_o[j].astype(bf16), zeros_d
        else:
            qkv = _qkv_proj(xf, na_w_qkv[j].astype(bf16), na_b_qkv[j][None], na_scale)
            rpb_pad = jnp.pad(na_rpb[j], ((0, 0), (0, 0), (0, LANES - na_rpb.shape[3])))
            o = _neighborhood_attention(qkv, rpb_pad, batch, seq)
            w_o, b_o = na_w_o[j].astype(bf16), na_b_o[j][None]
        xf = _out_proj_ln(o, xf, w_o, b_o, ln_g[i, 0][None], ln_b[i, 0][None])
        xf = _conv_ffn_ln(xf, seq, ffn_w_in[i].astype(bf16), ffn_b_in[i][None], ffn_conv_w[i],
                          ffn_conv_b[i][None], ffn_w_out[i].astype(bf16), ffn_b_out[i][None],
                          ln_g[i, 1][None], ln_b[i, 1][None])
    return xf.reshape(batch, seq, d)
```

```python
import functools
import math

import jax
import jax.numpy as jnp
from jax import lax
from jax.experimental import pallas as pl
from jax.experimental.pallas import tpu as pltpu

DEPTH = 4
HEAD_DIM = 64
GRID_W = 64
NA_KH = 8
NA_KW = 16
LN_EPS = 1e-5
RMS_EPS = 1e-5
ALPHA = (2.0 * DEPTH) ** 0.25

LANES = 128
VMEM_LIMIT_BYTES = 56 * 1024 * 1024

NEG_BIG = -1e30

PROJ_TM = 1024
PROJ_TN = 1024
OUT_TM = 512
FFN_TM = 1024
FFN_TN = 256
FFN_HALO = 16
FFN_RB = 256
ATT_TQ = 512
ATT_TK = 512
ATT_TK2 = 1024
POS_SPLIT = 256
NA_QROWS = 4
NA_KROWS = NA_QROWS + NA_KH
NA_TILES_PER_STEP = 2


def _cparams(sem):
    return pltpu.CompilerParams(dimension_semantics=sem, vmem_limit_bytes=VMEM_LIMIT_BYTES)


def _layer_norm_rows(z, g, b):
    mu = jnp.mean(z, axis=-1, keepdims=True)
    zc = z - mu
    var = jnp.mean(zc * zc, axis=-1, keepdims=True)
    return zc * lax.rsqrt(var + LN_EPS) * g + b


def _proj_kernel(x_ref, w_ref, b_ref, s_ref, o_ref):
    acc = jnp.dot(x_ref[...].astype(jnp.bfloat16), w_ref[...], preferred_element_type=jnp.float32)
    o_ref[...] = ((acc + b_ref[...]) * s_ref[...]).astype(o_ref.dtype)


def _qkv_proj(x, w, bias, colscale):
    m, d = x.shape
    n = w.shape[1]
    return pl.pallas_call(
        _proj_kernel,
        grid=(m // PROJ_TM, n // PROJ_TN),
        in_specs=[
            pl.BlockSpec((PROJ_TM, d), lambda i, j: (i, 0)),
            pl.BlockSpec((d, PROJ_TN), lambda i, j: (0, j)),
            pl.BlockSpec((1, PROJ_TN), lambda i, j: (0, j)),
            pl.BlockSpec((1, PROJ_TN), lambda i, j: (0, j)),
        ],
        out_specs=pl.BlockSpec((PROJ_TM, PROJ_TN), lambda i, j: (i, j)),
        out_shape=jax.ShapeDtypeStruct((m, n), jnp.bfloat16),
        compiler_params=_cparams(("parallel", "arbitrary")),
        name="qkv_proj",
    )(x, w, bias, colscale)


def _out_ln_kernel(o_ref, x_ref, w_ref, b_ref, g_ref, beta_ref, y_ref):
    mix = jnp.dot(o_ref[...], w_ref[...], preferred_element_type=jnp.float32) + b_ref[...]
    z = ALPHA * x_ref[...] + mix
    y_ref[...] = _layer_norm_rows(z, g_ref[...], beta_ref[...])


def _out_proj_ln(o, x, w, b, g, beta):
    m, d = x.shape
    k = o.shape[1]
    row = lambda i: (i, 0)
    fixed = lambda i: (0, 0)
    return pl.pallas_call(
        _out_ln_kernel,
        grid=(m // OUT_TM,),
        in_specs=[
            pl.BlockSpec((OUT_TM, k), row),
            pl.BlockSpec((OUT_TM, d), row),
            pl.BlockSpec((k, d), fixed),
            pl.BlockSpec((1, d), fixed),
            pl.BlockSpec((1, d), fixed),
            pl.BlockSpec((1, d), fixed),
        ],
        out_specs=pl.BlockSpec((OUT_TM, d), row),
        out_shape=jax.ShapeDtypeStruct((m, d), jnp.float32),
        compiler_params=_cparams(("parallel",)),
        name="out_proj_ln",
    )(o, x, w, b, g, beta)


def _alibi_features(n_heads, seq):
    assert 8 % n_heads == 0 and seq <= POS_SPLIT * POS_SPLIT
    m = jnp.exp2(-8.0 * jnp.arange(1, n_heads + 1, dtype=jnp.float32) / n_heads)[:, None]
    pos = jnp.arange(seq, dtype=jnp.int32)
    hi = ((pos // POS_SPLIT) * POS_SPLIT).astype(jnp.float32)[None]
    lo = (pos % POS_SPLIT).astype(jnp.float32)[None]
    one = jnp.ones((n_heads, seq), jnp.float32)
    pad = jnp.zeros((n_heads, seq, LANES - 4), jnp.float32)
    qf = jnp.concatenate([jnp.stack([-m * hi, -m * lo, one, one], axis=-1), pad], axis=-1)
    kf = jnp.concatenate([jnp.stack([one, one, m * hi, m * lo], axis=-1), pad], axis=-1)
    return qf.astype(jnp.bfloat16), kf.astype(jnp.bfloat16)


def _diff_attn_kernel(slopes_ref, q_ref, qf_ref, k_ref, kf_ref, v_ref, diag_ref, lam_ref, g_ref,
                      o_ref, ka_scr, va_scr, qa_scr, s_scr, m_scr, acc_scr, *, lambda_init, n_chunks):
    h = pl.program_id(1)
    qi = pl.program_id(2)
    tq = q_ref.shape[0]
    blocks = ATT_TK // LANES

    @pl.when(qi == 0)
    def _():
        ka_scr[:, :LANES] = k_ref[...]
        ka_scr[:, LANES:] = kf_ref[...]
        va_scr[:, :LANES] = v_ref[...]
        va_scr[:, LANES:] = jnp.ones(v_ref.shape, v_ref.dtype)

    q = q_ref[...]
    qf = qf_ref[...]
    lane = lax.broadcasted_iota(jnp.int32, q.shape, 1)
    zero = jnp.zeros_like(q)
    for c in range(2):
        qc = jnp.where((lane >= c * HEAD_DIM) & (lane < (c + 1) * HEAD_DIM), q, zero)
        qa_scr[2 * c] = jnp.concatenate([qc, qf], axis=1)
        qa_scr[2 * c + 1] = jnp.concatenate([qc, -qf], axis=1)
    m_scr[...] = jnp.full(m_scr.shape, -jnp.inf, jnp.float32)

    def score_chunk(j, after, on_diagonal):
        start = pl.multiple_of(j * ATT_TK, ATT_TK)
        kc = ka_scr[pl.ds(start, ATT_TK), :]
        for c in range(2):
            s = lax.dot_general(qa_scr[2 * c + after], kc, (((1,), (1,)), ((), ())),
                                preferred_element_type=jnp.float32)
            if on_diagonal:
                s = s - slopes_ref[h] * diag_ref[...]
            s_scr[c, j] = s
            mx = s[:, :LANES]
            for kb in range(1, blocks):
                mx = jnp.maximum(mx, s[:, kb * LANES:(kb + 1) * LANES])
            m_scr[c] = jnp.maximum(m_scr[c], mx)

    def before_body(j, carry):
        score_chunk(j, 0, False)
        return carry

    def after_body(j, carry):
        score_chunk(j, 1, False)
        return carry

    lax.fori_loop(0, qi, before_body, 0)
    score_chunk(qi, 0, True)
    lax.fori_loop(qi + 1, n_chunks, after_body, 0)

    for c in range(2):
        m_scr[c] = jnp.broadcast_to(jnp.max(m_scr[c], axis=-1, keepdims=True), (tq, LANES))
    acc_scr[...] = jnp.zeros_like(acc_scr)

    def pv_chunk(jj, carry):
        start = pl.multiple_of(jj * ATT_TK2, ATT_TK2)
        vc = va_scr[pl.ds(start, ATT_TK2), :]
        for c in range(2):
            mb = m_scr[c]
            parts = []
            for jsub in range(ATT_TK2 // ATT_TK):
                s = s_scr[c, jj * (ATT_TK2 // ATT_TK) + jsub]
                for kb in range(blocks):
                    parts.append(jnp.exp(s[:, kb * LANES:(kb + 1) * LANES] - mb).astype(jnp.bfloat16))
            p = jnp.concatenate(parts, axis=1)
            acc_scr[c] += jnp.dot(p, vc, preferred_element_type=jnp.float32)
        return carry

    lax.fori_loop(0, n_chunks * ATT_TK // ATT_TK2, pv_chunk, 0)

    lf = lam_ref[...]
    lam = (jnp.exp(jnp.sum(lf[0:1, :] * lf[1:2, :], axis=-1, keepdims=True))
           - jnp.exp(jnp.sum(lf[2:3, :] * lf[3:4, :], axis=-1, keepdims=True)) + lambda_init)
    a1 = acc_scr[0]
    a2 = acc_scr[1]
    o = a1[:, :LANES] / a1[:, LANES:] - lam * (a2[:, :LANES] / a2[:, LANES:])
    y = o * lax.rsqrt(jnp.mean(o * o, axis=-1, keepdims=True) + RMS_EPS) * g_ref[...]
    o_ref[...] = (y * (1.0 - lambda_init)).astype(o_ref.dtype)


def _diff_attention(qkv, slopes, qf, kf, diag, lam_vec, subln_g, batch, seq, lambda_init):
    n_heads = qkv.shape[1] // (3 * LANES)
    assert ATT_TQ == ATT_TK and seq % ATT_TK2 == 0 and ATT_TK2 % ATT_TK == 0
    nq = seq // ATT_TQ
    n_chunks = seq // ATT_TK
    kern = functools.partial(_diff_attn_kernel, lambda_init=lambda_init, n_chunks=n_chunks)
    fixed = lambda b, h, i: (0, 0)
    return pl.pallas_call(
        kern,
        grid=(batch, n_heads, nq),
        in_specs=[
            pl.BlockSpec(memory_space=pltpu.SMEM),
            pl.BlockSpec((ATT_TQ, LANES), lambda b, h, i: (b * nq + i, h)),
            pl.BlockSpec((None, ATT_TQ, LANES), lambda b, h, i: (h, i, 0)),
            pl.BlockSpec((seq, LANES), lambda b, h, i: (b, n_heads + h)),
            pl.BlockSpec((None, seq, LANES), lambda b, h, i: (h, 0, 0)),
            pl.BlockSpec((seq, LANES), lambda b, h, i: (b, 2 * n_heads + h)),
            pl.BlockSpec((ATT_TQ, ATT_TK), fixed),
            pl.BlockSpec((4, HEAD_DIM), fixed),
            pl.BlockSpec((1, LANES), fixed),
        ],
        out_specs=pl.BlockSpec((ATT_TQ, LANES), lambda b, h, i: (b * nq + i, h)),
        out_shape=jax.ShapeDtypeStruct((batch * seq, n_heads * LANES), jnp.bfloat16),
        scratch_shapes=[
            pltpu.VMEM((seq, 2 * LANES), jnp.bfloat16),
            pltpu.VMEM((seq, 2 * LANES), jnp.bfloat16),
            pltpu.VMEM((4, ATT_TQ, 2 * LANES), jnp.bfloat16),
            pltpu.VMEM((2, n_chunks, ATT_TQ, ATT_TK), jnp.float32),
            pltpu.VMEM((2, ATT_TQ, LANES), jnp.float32),
            pltpu.VMEM((2, ATT_TQ, 2 * LANES), jnp.float32),
        ],
        compiler_params=_cparams(("parallel", "parallel", "arbitrary")),
        name="diff_attn",
    )(slopes, qkv, qf, qkv, kf, qkv, diag, lam_vec, subln_g)


def _na_row_offset(cls, qr, kr):
    if cls == 0:
        return kr - qr + NA_KH - 1 if kr < NA_KH else None
    if cls == 2:
        return kr - qr - 1 if kr >= NA_KROWS - NA_KH else None
    return kr - qr + NA_KH // 2 - 1 if 0 <= kr - qr < NA_KH else None


def _na_build_bias(rpb_ref, bias_scr):
    shape = (GRID_W, LANES)
    qc = lax.broadcasted_iota(jnp.int32, shape, 0)
    lane = lax.broadcasted_iota(jnp.int32, shape, 1)
    kc = lane % GRID_W
    col_start = jnp.clip(qc - NA_KW // 2, 0, GRID_W - NA_KW)
    in_cols = (kc >= col_start) & (kc < col_start + NA_KW)
    left = lane < GRID_W
    neg = jnp.full(shape, NEG_BIG, jnp.float32)

    def toeplitz(hh, ro, lane_off):
        vec = jnp.broadcast_to(rpb_ref[hh, ro:ro + 1, :], shape)
        shift = (lane_off - (NA_KW - 1)) % LANES
        return pltpu.roll(vec, shift, 1, stride=1, stride_axis=0)

    for cls in range(3):
        for hh in range(2):
            for qr in range(NA_QROWS):
                for kp in range(NA_KROWS // 2):
                    ro_a = _na_row_offset(cls, qr, 2 * kp)
                    ro_b = _na_row_offset(cls, qr, 2 * kp + 1)
                    blk_a = neg if ro_a is None else jnp.where(in_cols, toeplitz(hh, ro_a, 0), neg)
                    blk_b = neg if ro_b is None else jnp.where(in_cols, toeplitz(hh, ro_b, GRID_W), neg)
                    bias_scr[cls, hh, qr * GRID_W:(qr + 1) * GRID_W, kp * LANES:(kp + 1) * LANES] = (
                        jnp.where(left, blk_a, blk_b))


def _na_kernel(q_ref, k_ref, v_ref, rpb_ref, o_ref, bias_scr, *, n_tiles):
    b = pl.program_id(1)
    t = pl.program_id(2)

    @pl.when((b == 0) & (t == 0))
    def _():
        _na_build_bias(rpb_ref, bias_scr)

    n_keys = NA_KROWS * GRID_W
    tq = NA_QROWS * GRID_W
    rows_total = n_tiles * NA_QROWS
    lane = lax.broadcasted_iota(jnp.int32, (tq, LANES), 1)
    ones = jnp.ones((n_keys, LANES), jnp.bfloat16)

    for sub in range(NA_TILES_PER_STEP):
        tile = t * NA_TILES_PER_STEP + sub
        ws = jnp.clip(tile * NA_QROWS - NA_KH // 2, 0, rows_total - NA_KROWS)
        start = pl.multiple_of(ws * GRID_W, GRID_W)
        kw = k_ref[pl.ds(start, n_keys), :]
        vw = jnp.concatenate([v_ref[pl.ds(start, n_keys), :], ones], axis=1)
        cls = jnp.where(tile == 0, 0, jnp.where(tile == n_tiles - 1, 2, 1))

        q = q_ref[sub * tq:(sub + 1) * tq, :]
        zero = jnp.zeros_like(q)
        outs = []
        for hh in range(2):
            qm = jnp.where((lane >= hh * HEAD_DIM) & (lane < (hh + 1) * HEAD_DIM), q, zero)
            s = lax.dot_general(qm, kw, (((1,), (1,)), ((), ())), preferred_element_type=jnp.float32)
            s = s + bias_scr[cls, hh]
            p = jnp.exp(s - jnp.max(s, axis=-1, keepdims=True))
            pv = jnp.dot(p.astype(jnp.bfloat16), vw, preferred_element_type=jnp.float32)
            outs.append(pv[:, :LANES] / pv[:, LANES:])
        o_ref[sub * tq:(sub + 1) * tq, :] = jnp.where(lane < HEAD_DIM, outs[0], outs[1]).astype(o_ref.dtype)


def _neighborhood_attention(qkv, rpb_pad, batch, seq):
    n_pairs = qkv.shape[1] // (3 * LANES)
    tq = NA_QROWS * GRID_W
    n_tiles = seq // tq
    n_steps = n_tiles // NA_TILES_PER_STEP
    blk = NA_TILES_PER_STEP * tq
    assert seq // GRID_W >= NA_KROWS and seq % blk == 0
    kern = functools.partial(_na_kernel, n_tiles=n_tiles)
    return pl.pallas_call(
        kern,
        grid=(n_pairs, batch, n_steps),
        in_specs=[
            pl.BlockSpec((blk, LANES), lambda hp, b, t: (b * n_steps + t, hp)),
            pl.BlockSpec((seq, LANES), lambda hp, b, t: (b, n_pairs + hp)),
            pl.BlockSpec((seq, LANES), lambda hp, b, t: (b, 2 * n_pairs + hp)),
            pl.BlockSpec((2, 2 * NA_KH - 1, LANES), lambda hp, b, t: (hp, 0, 0)),
        ],
        out_specs=pl.BlockSpec((blk, LANES), lambda hp, b, t: (b * n_steps + t, hp)),
        out_shape=jax.ShapeDtypeStruct((batch * seq, n_pairs * LANES), jnp.bfloat16),
        scratch_shapes=[pltpu.VMEM((3, 2, tq, NA_KROWS * GRID_W), jnp.float32)],
        compiler_params=_cparams(("arbitrary", "arbitrary", "arbitrary")),
        name="na_attn",
    )(qkv, qkv, qkv, rpb_pad)


def _ffn_kernel(x_ref, xp_ref, xn_ref, wu_ref, wg_ref, bu_ref, bg_ref, cwu_ref, cwg_ref,
                cbu_ref, cbg_ref, wo_ref, bo_ref, g_ref, beta_ref, y_ref, xb_scr, hu_scr, hg_scr,
                acc_scr, *, tiles_per_seq):
    i = pl.program_id(0)
    j = pl.program_id(1)
    tm = x_ref.shape[0]
    halo = FFN_HALO

    @pl.when(j == 0)
    def _():
        xb_scr[:halo, :] = xp_ref[...].astype(jnp.bfloat16)
        xb_scr[halo:halo + tm, :] = x_ref[...].astype(jnp.bfloat16)
        xb_scr[halo + tm:, :] = xn_ref[...].astype(jnp.bfloat16)
        acc_scr[...] = jnp.zeros_like(acc_scr)

    has_prev = (i % tiles_per_seq != 0).astype(jnp.float32)
    has_next = (i % tiles_per_seq != tiles_per_seq - 1).astype(jnp.float32)
    n_sub = tm // FFN_RB

    def up_proj(r, w_ref, b_ref, h_scr):
        lo = 0 if r == 0 else r * FFN_RB + 2 * halo
        hi = (r + 1) * FFN_RB + 2 * halo
        h = jnp.dot(xb_scr[lo:hi, :], w_ref[...], preferred_element_type=jnp.float32) + b_ref[...]
        if r == 0:
            h_scr[:halo, :] = h[:halo] * has_prev
            h_scr[halo:hi, :] = h[halo:]
        elif r == n_sub - 1:
            h_scr[lo:hi - halo, :] = h[:hi - halo - lo]
            h_scr[hi - halo:hi, :] = h[hi - halo - lo:] * has_next
        else:
            h_scr[lo:hi, :] = h

    def conv(r, cw_ref, cb_ref, h_scr):
        cw = cw_ref[...]
        base = r * FFN_RB + halo
        return (h_scr[pl.ds(base - 1, FFN_RB), :] * cw[0:1, :] + h_scr[pl.ds(base, FFN_RB), :] * cw[1:2, :]
                + h_scr[pl.ds(base + 1, FFN_RB), :] * cw[2:3, :] + cb_ref[...])

    up_proj(0, wu_ref, bu_ref, hu_scr)
    up_proj(0, wg_ref, bg_ref, hg_scr)
    for r in range(n_sub):
        if r + 1 < n_sub:
            up_proj(r + 1, wu_ref, bu_ref, hu_scr)
            up_proj(r + 1, wg_ref, bg_ref, hg_scr)
        u = conv(r, cwu_ref, cbu_ref, hu_scr)
        g = conv(r, cwg_ref, cbg_ref, hg_scr)
        gelu = 0.5 * g * (1.0 + lax.erf(g * (1.0 / math.sqrt(2.0))))
        a = (u * gelu).astype(jnp.bfloat16)
        acc_scr[r * FFN_RB:(r + 1) * FFN_RB, :] += jnp.dot(a, wo_ref[...],
                                                         preferred_element_type=jnp.float32)

    @pl.when(j == pl.num_programs(1) - 1)
    def _():
        z = ALPHA * x_ref[...] + acc_scr[...] + bo_ref[...]
        y_ref[...] = _layer_norm_rows(z, g_ref[...], beta_ref[...])


def _conv_ffn_ln(x, seq, w_in, b_in, conv_w, conv_b, w_out, b_out, g, beta):
    m, d = x.shape
    d_ff = w_out.shape[0]
    nj = d_ff // FFN_TN
    n_i = m // FFN_TM
    tiles_per_seq = seq // FFN_TM
    assert seq % FFN_TM == 0 and FFN_TM % FFN_RB == 0 and FFN_TM // FFN_RB >= 2
    halo_per_tile = FFN_TM // FFN_HALO
    n_halo_blocks = m // FFN_HALO
    u_col = lambda i, j: (0, j)
    g_col = lambda i, j: (0, nj + j)
    fixed = lambda i, j: (0, 0)
    kern = functools.partial(_ffn_kernel, tiles_per_seq=tiles_per_seq)
    return pl.pallas_call(
        kern,
        grid=(n_i, nj),
        in_specs=[
            pl.BlockSpec((FFN_TM, d), lambda i, j: (i, 0)),
            pl.BlockSpec((FFN_HALO, d), lambda i, j: (jnp.maximum(i * halo_per_tile - 1, 0), 0)),
            pl.BlockSpec((FFN_HALO, d),
                         lambda i, j: (jnp.minimum((i + 1) * halo_per_tile, n_halo_blocks - 1), 0)),
            pl.BlockSpec((d, FFN_TN), u_col),
            pl.BlockSpec((d, FFN_TN), g_col),
            pl.BlockSpec((1, FFN_TN), u_col),
            pl.BlockSpec((1, FFN_TN), g_col),
            pl.BlockSpec((3, FFN_TN), u_col),
            pl.BlockSpec((3, FFN_TN), g_col),
            pl.BlockSpec((1, FFN_TN), u_col),
            pl.BlockSpec((1, FFN_TN), g_col),
            pl.BlockSpec((FFN_TN, d), lambda i, j: (j, 0)),
            pl.BlockSpec((1, d), fixed),
            pl.BlockSpec((1, d), fixed),
            pl.BlockSpec((1, d), fixed),
        ],
        out_specs=pl.BlockSpec((FFN_TM, d), lambda i, j: (i, 0)),
        out_shape=jax.ShapeDtypeStruct((m, d), jnp.float32),
        scratch_shapes=[
            pltpu.VMEM((FFN_TM + 2 * FFN_HALO, d), jnp.bfloat16),
            pltpu.VMEM((FFN_TM + 2 * FFN_HALO, FFN_TN), jnp.float32),
            pltpu.VMEM((FFN_TM + 2 * FFN_HALO, FFN_TN), jnp.float32),
            pltpu.VMEM((FFN_TM, d), jnp.float32),
        ],
        compiler_params=_cparams(("parallel", "arbitrary")),
        name="conv_ffn_ln",
    )(x, x, x, w_in, w_in, b_in, b_in, conv_w, conv_w, conv_b, conv_b, w_out, b_out, g, beta)


def kernel(x, attn_w_qkv, attn_w_o, attn_lambda, attn_subln_g, na_w_qkv, na_b_qkv, na_rpb,
           na_w_o, na_b_o, ffn_w_in, ffn_b_in, ffn_conv_w, ffn_conv_b, ffn_w_out, ffn_b_out,
           ln_g, ln_b):
    batch, seq, d = x.shape
    bf16 = jnp.bfloat16
    f32 = jnp.float32
    xf = x.reshape(batch * seq, d)
    q_scale = HEAD_DIM ** -0.5

    n_attn_heads = attn_w_qkv.shape[2] // (3 * 2 * HEAD_DIM)
    slopes = jnp.exp2(-8.0 * jnp.arange(1, n_attn_heads + 1, dtype=f32) / n_attn_heads)
    qk_cols = 2 * n_attn_heads * HEAD_DIM
    attn_scale = jnp.where(jnp.arange(attn_w_qkv.shape[2]) < qk_cols, q_scale, 1.0).astype(f32)[None]
    na_dim = na_w_qkv.shape[2] // 3
    na_scale = jnp.where(jnp.arange(3 * na_dim) < na_dim, q_scale, 1.0).astype(f32)[None]
    zeros_d = jnp.zeros((1, d), f32)
    qf, kf = _alibi_features(n_attn_heads, seq)
    rel = jnp.arange(ATT_TK, dtype=f32)[None, :] - jnp.arange(ATT_TQ, dtype=f32)[:, None]
    diag = 2.0 * jnp.maximum(rel, 0.0)

    for i in range(DEPTH):
        j = i // 2
        if i % 2 == 0:
            lambda_init = 0.8 - 0.6 * math.exp(-0.3 * i)
            w = attn_w_qkv[j].astype(bf16)
            qkv = _qkv_proj(xf, w, jnp.zeros((1, w.shape[1]), f32), attn_scale)
            o = _diff_attention(qkv, slopes, qf, kf, diag, attn_lambda[j], attn_subln_g[j][None],
                                batch, seq, lambda_init)
            w_o, b_o = attn_w_o[j].astype(bf16), zeros_d
        else:
            qkv = _qkv_proj(xf, na_w_qkv[j].astype(bf16), na_b_qkv[j][None], na_scale)
            rpb_pad = jnp.pad(na_rpb[j], ((0, 0), (0, 0), (0, LANES - na_rpb.shape[3])))
            o = _neighborhood_attention(qkv, rpb_pad, batch, seq)
            w_o, b_o = na_w_o[j].astype(bf16), na_b_o[j][None]
        xf = _out_proj_ln(o, xf, w_o, b_o, ln_g[i, 0][None], ln_b[i, 0][None])
        xf = _conv_ffn_ln(xf, seq, ffn_w_in[i].astype(bf16), ffn_b_in[i][None], ffn_conv_w[i],
                          ffn_conv_b[i][None], ffn_w_out[i].astype(bf16), ffn_b_out[i][None],
                          ln_g[i, 1][None], ln_b[i, 1][None])
    return xf.reshape(batch, seq, d)
```

```python
import functools
import math

import jax
import jax.numpy as jnp
from jax import lax
from jax.experimental import pallas as pl
from jax.experimental.pallas import tpu as pltpu

DEPTH = 4
HEAD_DIM = 64
GRID_W = 64
NA_KH = 8
NA_KW = 16
LN_EPS = 1e-5
RMS_EPS = 1e-5
ALPHA = (2.0 * DEPTH) ** 0.25

LANES = 128
VMEM_LIMIT_BYTES = 56 * 1024 * 1024

NEG_BIG = -1e30

PROJ_TM = 1024
PROJ_TN = 1024
OUT_TM = 512
FFN_TM = 1024
FFN_TN = 256
FFN_HALO = 16
FFN_RB = 256
ATT_TQ = 512
ATT_TK = 1024
ATT_UNROLL = 4
POS_SPLIT = 256
NA_QROWS = 4
NA_KROWS = NA_QROWS + NA_KH
NA_TILES_PER_STEP = 4


def _cparams(sem):
    return pltpu.CompilerParams(dimension_semantics=sem, vmem_limit_bytes=VMEM_LIMIT_BYTES)


def _layer_norm_rows(z, g, b):
    mu = jnp.mean(z, axis=-1, keepdims=True)
    zc = z - mu
    var = jnp.mean(zc * zc, axis=-1, keepdims=True)
    return zc * lax.rsqrt(var + LN_EPS) * g + b


def _proj_kernel(x_ref, w_ref, b_ref, s_ref, o_ref):
    acc = jnp.dot(x_ref[...].astype(jnp.bfloat16), w_ref[...], preferred_element_type=jnp.float32)
    o_ref[...] = ((acc + b_ref[...]) * s_ref[...]).astype(o_ref.dtype)


def _qkv_proj(x, w, bias, colscale):
    m, d = x.shape
    n = w.shape[1]
    return pl.pallas_call(
        _proj_kernel,
        grid=(m // PROJ_TM, n // PROJ_TN),
        in_specs=[
            pl.BlockSpec((PROJ_TM, d), lambda i, j: (i, 0)),
            pl.BlockSpec((d, PROJ_TN), lambda i, j: (0, j)),
            pl.BlockSpec((1, PROJ_TN), lambda i, j: (0, j)),
            pl.BlockSpec((1, PROJ_TN), lambda i, j: (0, j)),
        ],
        out_specs=pl.BlockSpec((PROJ_TM, PROJ_TN), lambda i, j: (i, j)),
        out_shape=jax.ShapeDtypeStruct((m, n), jnp.bfloat16),
        compiler_params=_cparams(("parallel", "arbitrary")),
        name="qkv_proj",
    )(x, w, bias, colscale)


def _out_ln_kernel(o_ref, x_ref, w_ref, b_ref, g_ref, beta_ref, y_ref):
    mix = jnp.dot(o_ref[...], w_ref[...], preferred_element_type=jnp.float32) + b_ref[...]
    z = ALPHA * x_ref[...] + mix
    y_ref[...] = _layer_norm_rows(z, g_ref[...], beta_ref[...])


def _out_proj_ln(o, x, w, b, g, beta):
    m, d = x.shape
    k = o.shape[1]
    row = lambda i: (i, 0)
    fixed = lambda i: (0, 0)
    return pl.pallas_call(
        _out_ln_kernel,
        grid=(m // OUT_TM,),
        in_specs=[
            pl.BlockSpec((OUT_TM, k), row),
            pl.BlockSpec((OUT_TM, d), row),
            pl.BlockSpec((k, d), fixed),
            pl.BlockSpec((1, d), fixed),
            pl.BlockSpec((1, d), fixed),
            pl.BlockSpec((1, d), fixed),
        ],
        out_specs=pl.BlockSpec((OUT_TM, d), row),
        out_shape=jax.ShapeDtypeStruct((m, d), jnp.float32),
        compiler_params=_cparams(("parallel",)),
        name="out_proj_ln",
    )(o, x, w, b, g, beta)


def _alibi_features(n_heads, seq):
    assert 8 % n_heads == 0 and seq <= POS_SPLIT * POS_SPLIT
    m = jnp.exp2(-8.0 * jnp.arange(1, n_heads + 1, dtype=jnp.float32) / n_heads)[:, None]
    pos = jnp.arange(seq, dtype=jnp.int32)
    hi = ((pos // POS_SPLIT) * POS_SPLIT).astype(jnp.float32)[None]
    lo = (pos % POS_SPLIT).astype(jnp.float32)[None]
    one = jnp.ones((n_heads, seq), jnp.float32)
    pad = jnp.zeros((n_heads, seq, LANES - 4), jnp.float32)
    qf = jnp.concatenate([jnp.stack([-m * hi, -m * lo, one, one], axis=-1), pad], axis=-1)
    kf = jnp.concatenate([jnp.stack([one, one, m * hi, m * lo], axis=-1), pad], axis=-1)
    return qf.astype(jnp.bfloat16), kf.astype(jnp.bfloat16)


def _diag_correction():
    t = jnp.arange(ATT_TQ, dtype=jnp.float32)[None, :, None]
    s = jnp.arange(ATT_TK, dtype=jnp.float32)[None, None, :]
    off = (jnp.arange(ATT_TK // ATT_TQ, dtype=jnp.float32) * ATT_TQ)[:, None, None]
    return 2.0 * jnp.maximum(s - t - off, 0.0)


def _diff_attn_kernel(slopes_ref, q_ref, qf_ref, k_ref, kf_ref, v_ref, diag_ref, lam_ref, g_ref,
                      o_ref, ka_scr, va_scr, qa_scr, s0_scr, s1_scr, mrun_scr, mfin_scr, acc_scr,
                      *, lambda_init, n_chunks, n_tiles):
    h = pl.program_id(1)
    i = pl.program_id(2)
    tq = q_ref.shape[0]
    diag_chunk = i // (ATT_TK // ATT_TQ)
    blocks = ATT_TK // LANES
    s_scrs = (s0_scr, s1_scr)

    @pl.when(i == 0)
    def _():
        ka_scr[:, :LANES] = k_ref[...]
        ka_scr[:, LANES:] = kf_ref[...]
        va_scr[:, :LANES] = v_ref[...]
        va_scr[:, LANES:] = jnp.ones(v_ref.shape, v_ref.dtype)
        acc_scr[...] = jnp.zeros_like(acc_scr)

    @pl.when(i < n_tiles)
    def _():
        q = q_ref[...]
        qf = qf_ref[...]
        lane = lax.broadcasted_iota(jnp.int32, q.shape, 1)
        zero = jnp.zeros_like(q)
        for c in range(2):
            qc = jnp.where((lane >= c * HEAD_DIM) & (lane < (c + 1) * HEAD_DIM), q, zero)
            qa_scr[2 * c] = jnp.concatenate([qc, qf], axis=1)
            qa_scr[2 * c + 1] = jnp.concatenate([qc, -qf], axis=1)
        mrun_scr[...] = jnp.full(mrun_scr.shape, -jnp.inf, jnp.float32)

    def score_one(c, j):
        start = pl.multiple_of(j * ATT_TK, ATT_TK)
        after = (j > diag_chunk).astype(jnp.int32)
        s = lax.dot_general(qa_scr[2 * c + after], ka_scr[pl.ds(start, ATT_TK), :],
                            (((1,), (1,)), ((), ())), preferred_element_type=jnp.float32)
        coef = jnp.where(j == diag_chunk, slopes_ref[h], 0.0)
        s = s - coef * diag_ref[...]
        s_scrs[c][j] = s
        mx = s[:, :LANES]
        for kb in range(1, blocks):
            mx = jnp.maximum(mx, s[:, kb * LANES:(kb + 1) * LANES])
        mrun_scr[c] = jnp.maximum(mrun_scr[c], mx)

    def pv_one(c, j):
        start = pl.multiple_of(j * ATT_TK, ATT_TK)
        s = s_scrs[c][j]
        mb = mfin_scr[c]
        p = jnp.concatenate(
            [jnp.exp(s[:, kb * LANES:(kb + 1) * LANES] - mb).astype(jnp.bfloat16) for kb in range(blocks)],
            axis=1)
        acc_scr[c] += jnp.dot(p, va_scr[pl.ds(start, ATT_TK), :], preferred_element_type=jnp.float32)

    def run_chunks(score_c, pv_c):
        def body(jj, carry):
            for u in range(ATT_UNROLL):
                j = jj * ATT_UNROLL + u
                if score_c is not None:
                    score_one(score_c, j)
                if pv_c is not None:
                    pv_one(pv_c, j)
            return carry
        lax.fori_loop(0, n_chunks // ATT_UNROLL, body, 0)

    def finish_max(c):
        mfin_scr[c] = jnp.broadcast_to(jnp.max(mrun_scr[c], axis=-1, keepdims=True), (tq, LANES))

    @pl.when(i == 0)
    def _():
        run_chunks(0, None)

    @pl.when((i > 0) & (i < n_tiles))
    def _():
        run_chunks(0, 1)

    @pl.when(i == n_tiles)
    def _():
        run_chunks(None, 1)

    @pl.when(i > 0)
    def _():
        lf = lam_ref[...]
        lam = (jnp.exp(jnp.sum(lf[0:1, :] * lf[1:2, :], axis=-1, keepdims=True))
               - jnp.exp(jnp.sum(lf[2:3, :] * lf[3:4, :], axis=-1, keepdims=True)) + lambda_init)
        a1 = acc_scr[0]
        a2 = acc_scr[1]
        o = a1[:, :LANES] / a1[:, LANES:] - lam * (a2[:, :LANES] / a2[:, LANES:])
        y = o * lax.rsqrt(jnp.mean(o * o, axis=-1, keepdims=True) + RMS_EPS) * g_ref[...]
        o_ref[...] = (y * (1.0 - lambda_init)).astype(o_ref.dtype)
        acc_scr[...] = jnp.zeros_like(acc_scr)

    @pl.when(i < n_tiles)
    def _():
        finish_max(0)
        run_chunks(1, 0)
        finish_max(1)


def _diff_attention(qkv, slopes, qf, kf, diag, lam_vec, subln_g, batch, seq, lambda_init):
    n_heads = qkv.shape[1] // (3 * LANES)
    assert ATT_TK % ATT_TQ == 0 and seq % (ATT_TK * ATT_UNROLL) == 0
    nq = seq // ATT_TQ
    n_chunks = seq // ATT_TK
    tiles_per_chunk = ATT_TK // ATT_TQ
    kern = functools.partial(_diff_attn_kernel, lambda_init=lambda_init, n_chunks=n_chunks, n_tiles=nq)
    fixed = lambda b, h, i: (0, 0)
    scored = lambda i: jnp.minimum(i, nq - 1)
    finished = lambda i: jnp.maximum(i - 1, 0)
    return pl.pallas_call(
        kern,
        grid=(batch, n_heads, nq + 1),
        in_specs=[
            pl.BlockSpec(memory_space=pltpu.SMEM),
            pl.BlockSpec((ATT_TQ, LANES), lambda b, h, i: (b * nq + scored(i), h)),
            pl.BlockSpec((None, ATT_TQ, LANES), lambda b, h, i: (h, scored(i), 0)),
            pl.BlockSpec((seq, LANES), lambda b, h, i: (b, n_heads + h)),
            pl.BlockSpec((None, seq, LANES), lambda b, h, i: (h, 0, 0)),
            pl.BlockSpec((seq, LANES), lambda b, h, i: (b, 2 * n_heads + h)),
            pl.BlockSpec((None, ATT_TQ, ATT_TK), lambda b, h, i: (scored(i) % tiles_per_chunk, 0, 0)),
            pl.BlockSpec((4, HEAD_DIM), fixed),
            pl.BlockSpec((1, LANES), fixed),
        ],
        out_specs=pl.BlockSpec((ATT_TQ, LANES), lambda b, h, i: (b * nq + finished(i), h)),
        out_shape=jax.ShapeDtypeStruct((batch * seq, n_heads * LANES), jnp.bfloat16),
        scratch_shapes=[
            pltpu.VMEM((seq, 2 * LANES), jnp.bfloat16),
            pltpu.VMEM((seq, 2 * LANES), jnp.bfloat16),
            pltpu.VMEM((4, ATT_TQ, 2 * LANES), jnp.bfloat16),
            pltpu.VMEM((n_chunks, ATT_TQ, ATT_TK), jnp.float32),
            pltpu.VMEM((n_chunks, ATT_TQ, ATT_TK), jnp.float32),
            pltpu.VMEM((2, ATT_TQ, LANES), jnp.float32),
            pltpu.VMEM((2, ATT_TQ, LANES), jnp.float32),
            pltpu.VMEM((2, ATT_TQ, 2 * LANES), jnp.float32),
        ],
        compiler_params=_cparams(("parallel", "parallel", "arbitrary")),
        name="diff_attn",
    )(slopes, qkv, qf, qkv, kf, qkv, diag, lam_vec, subln_g)


def _na_row_offset(cls, qr, kr):
    if cls == 0:
        return kr - qr + NA_KH - 1 if kr < NA_KH else None
    if cls == 2:
        return kr - qr - 1 if kr >= NA_KROWS - NA_KH else None
    return kr - qr + NA_KH // 2 - 1 if 0 <= kr - qr < NA_KH else None


def _na_build_bias(rpb_ref, bias_scr):
    shape = (GRID_W, LANES)
    qc = lax.broadcasted_iota(jnp.int32, shape, 0)
    lane = lax.broadcasted_iota(jnp.int32, shape, 1)
    kc = lane % GRID_W
    col_start = jnp.clip(qc - NA_KW // 2, 0, GRID_W - NA_KW)
    in_cols = (kc >= col_start) & (kc < col_start + NA_KW)
    left = lane < GRID_W
    neg = jnp.full(shape, NEG_BIG, jnp.float32)

    def toeplitz(hh, ro, lane_off):
        vec = jnp.broadcast_to(rpb_ref[hh, ro:ro + 1, :], shape)
        shift = (lane_off - (NA_KW - 1)) % LANES
        return pltpu.roll(vec, shift, 1, stride=1, stride_axis=0)

    for cls in range(3):
        for hh in range(2):
            for qr in range(NA_QROWS):
                for kp in range(NA_KROWS // 2):
                    ro_a = _na_row_offset(cls, qr, 2 * kp)
                    ro_b = _na_row_offset(cls, qr, 2 * kp + 1)
                    blk_a = neg if ro_a is None else jnp.where(in_cols, toeplitz(hh, ro_a, 0), neg)
                    blk_b = neg if ro_b is None else jnp.where(in_cols, toeplitz(hh, ro_b, GRID_W), neg)
                    bias_scr[cls, hh, qr * GRID_W:(qr + 1) * GRID_W, kp * LANES:(kp + 1) * LANES] = (
                        jnp.where(left, blk_a, blk_b))


def _na_kernel(q_ref, k_ref, v_ref, rpb_ref, o_ref, bias_scr, *, n_tiles):
    b = pl.program_id(1)
    t = pl.program_id(2)

    @pl.when((b == 0) & (t == 0))
    def _():
        _na_build_bias(rpb_ref, bias_scr)

    n_keys = NA_KROWS * GRID_W
    tq = NA_QROWS * GRID_W
    rows_total = n_tiles * NA_QROWS
    lane = lax.broadcasted_iota(jnp.int32, (tq, LANES), 1)
    ones = jnp.ones((n_keys, LANES), jnp.bfloat16)

    for sub in range(NA_TILES_PER_STEP):
        tile = t * NA_TILES_PER_STEP + sub
        ws = jnp.clip(tile * NA_QROWS - NA_KH // 2, 0, rows_total - NA_KROWS)
        start = pl.multiple_of(ws * GRID_W, GRID_W)
        kw = k_ref[pl.ds(start, n_keys), :]
        vw = jnp.concatenate([v_ref[pl.ds(start, n_keys), :], ones], axis=1)
        cls = jnp.where(tile == 0, 0, jnp.where(tile == n_tiles - 1, 2, 1))

        q = q_ref[sub * tq:(sub + 1) * tq, :]
        zero = jnp.zeros_like(q)
        outs = []
        for hh in range(2):
            qm = jnp.where((lane >= hh * HEAD_DIM) & (lane < (hh + 1) * HEAD_DIM), q, zero)
            s = lax.dot_general(qm, kw, (((1,), (1,)), ((), ())), preferred_element_type=jnp.float32)
            s = s + bias_scr[cls, hh]
            p = jnp.exp(s - jnp.max(s, axis=-1, keepdims=True))
            pv = jnp.dot(p.astype(jnp.bfloat16), vw, preferred_element_type=jnp.float32)
            outs.append(pv[:, :LANES] / pv[:, LANES:])
        o_ref[sub * tq:(sub + 1) * tq, :] = jnp.where(lane < HEAD_DIM, outs[0], outs[1]).astype(o_ref.dtype)


def _neighborhood_attention(qkv, rpb_pad, batch, seq):
    n_pairs = qkv.shape[1] // (3 * LANES)
    tq = NA_QROWS * GRID_W
    n_tiles = seq // tq
    n_steps = n_tiles // NA_TILES_PER_STEP
    blk = NA_TILES_PER_STEP * tq
    assert seq // GRID_W >= NA_KROWS and seq % blk == 0
    kern = functools.partial(_na_kernel, n_tiles=n_tiles)
    return pl.pallas_call(
        kern,
        grid=(n_pairs, batch, n_steps),
        in_specs=[
            pl.BlockSpec((blk, LANES), lambda hp, b, t: (b * n_steps + t, hp)),
            pl.BlockSpec((seq, LANES), lambda hp, b, t: (b, n_pairs + hp)),
            pl.BlockSpec((seq, LANES), lambda hp, b, t: (b, 2 * n_pairs + hp)),
            pl.BlockSpec((2, 2 * NA_KH - 1, LANES), lambda hp, b, t: (hp, 0, 0)),
        ],
        out_specs=pl.BlockSpec((blk, LANES), lambda hp, b, t: (b * n_steps + t, hp)),
        out_shape=jax.ShapeDtypeStruct((batch * seq, n_pairs * LANES), jnp.bfloat16),
        scratch_shapes=[pltpu.VMEM((3, 2, tq, NA_KROWS * GRID_W), jnp.float32)],
        compiler_params=_cparams(("arbitrary", "arbitrary", "arbitrary")),
        name="na_attn",
    )(qkv, qkv, qkv, rpb_pad)


def _ffn_kernel(x_ref, xp_ref, xn_ref, wu_ref, wg_ref, bu_ref, bg_ref, cwu_ref, cwg_ref,
                cbu_ref, cbg_ref, wo_ref, bo_ref, g_ref, beta_ref, y_ref, xb_scr, hu_scr, hg_scr,
                acc_scr, *, tiles_per_seq):
    i = pl.program_id(0)
    j = pl.program_id(1)
    tm = x_ref.shape[0]
    halo = FFN_HALO

    @pl.when(j == 0)
    def _():
        xb_scr[:halo, :] = xp_ref[...].astype(jnp.bfloat16)
        xb_scr[halo:halo + tm, :] = x_ref[...].astype(jnp.bfloat16)
        xb_scr[halo + tm:, :] = xn_ref[...].astype(jnp.bfloat16)
        acc_scr[...] = jnp.zeros_like(acc_scr)

    has_prev = (i % tiles_per_seq != 0).astype(jnp.float32)
    has_next = (i % tiles_per_seq != tiles_per_seq - 1).astype(jnp.float32)
    n_sub = tm // FFN_RB

    def up_proj(r, w_ref, b_ref, h_scr):
        lo = 0 if r == 0 else r * FFN_RB + 2 * halo
        hi = (r + 1) * FFN_RB + 2 * halo
        h = jnp.dot(xb_scr[lo:hi, :], w_ref[...], preferred_element_type=jnp.float32) + b_ref[...]
        if r == 0:
            h_scr[:halo, :] = h[:halo] * has_prev
            h_scr[halo:hi, :] = h[halo:]
        elif r == n_sub - 1:
            h_scr[lo:hi - halo, :] = h[:hi - halo - lo]
            h_scr[hi - halo:hi, :] = h[hi - halo - lo:] * has_next
        else:
            h_scr[lo:hi, :] = h

    def conv(r, cw_ref, cb_ref, h_scr):
        cw = cw_ref[...]
        lo = r * FFN_RB + halo - 8
        win = h_scr[lo:lo + FFN_RB + 16, :]
        prev = pltpu.roll(win, 1, 0)[8:8 + FFN_RB]
        nxt = pltpu.roll(win, FFN_RB + 15, 0)[8:8 + FFN_RB]
        return prev * cw[0:1, :] + win[8:8 + FFN_RB] * cw[1:2, :] + nxt * cw[2:3, :] + cb_ref[...]

    up_proj(0, wu_ref, bu_ref, hu_scr)
    up_proj(0, wg_ref, bg_ref, hg_scr)
    for r in range(n_sub):
        if r + 1 < n_sub:
            up_proj(r + 1, wu_ref, bu_ref, hu_scr)
            up_proj(r + 1, wg_ref, bg_ref, hg_scr)
        u = conv(r, cwu_ref, cbu_ref, hu_scr)
        g = conv(r, cwg_ref, cbg_ref, hg_scr)
        gelu = 0.5 * g * (1.0 + lax.erf(g * (1.0 / math.sqrt(2.0))))
        a = (u * gelu).astype(jnp.bfloat16)
        acc_scr[r * FFN_RB:(r + 1) * FFN_RB, :] += jnp.dot(a, wo_ref[...],
                                                         preferred_element_type=jnp.float32)

    @pl.when(j == pl.num_programs(1) - 1)
    def _():
        z = ALPHA * x_ref[...] + acc_scr[...] + bo_ref[...]
        y_ref[...] = _layer_norm_rows(z, g_ref[...], beta_ref[...])


def _conv_ffn_ln(x, seq, w_in, b_in, conv_w, conv_b, w_out, b_out, g, beta):
    m, d = x.shape
    d_ff = w_out.shape[0]
    nj = d_ff // FFN_TN
    n_i = m // FFN_TM
    tiles_per_seq = seq // FFN_TM
    assert seq % FFN_TM == 0 and FFN_TM % FFN_RB == 0 and FFN_TM // FFN_RB >= 2
    halo_per_tile = FFN_TM // FFN_HALO
    n_halo_blocks = m // FFN_HALO
    u_col = lambda i, j: (0, j)
    g_col = lambda i, j: (0, nj + j)
    fixed = lambda i, j: (0, 0)
    kern = functools.partial(_ffn_kernel, tiles_per_seq=tiles_per_seq)
    return pl.pallas_call(
        kern,
        grid=(n_i, nj),
        in_specs=[
            pl.BlockSpec((FFN_TM, d), lambda i, j: (i, 0)),
            pl.BlockSpec((FFN_HALO, d), lambda i, j: (jnp.maximum(i * halo_per_tile - 1, 0), 0)),
            pl.BlockSpec((FFN_HALO, d),
                         lambda i, j: (jnp.minimum((i + 1) * halo_per_tile, n_halo_blocks - 1), 0)),
            pl.BlockSpec((d, FFN_TN), u_col),
            pl.BlockSpec((d, FFN_TN), g_col),
            pl.BlockSpec((1, FFN_TN), u_col),
            pl.BlockSpec((1, FFN_TN), g_col),
            pl.BlockSpec((3, FFN_TN), u_col),
            pl.BlockSpec((3, FFN_TN), g_col),
            pl.BlockSpec((1, FFN_TN), u_col),
            pl.BlockSpec((1, FFN_TN), g_col),
            pl.BlockSpec((FFN_TN, d), lambda i, j: (j, 0)),
            pl.BlockSpec((1, d), fixed),
            pl.BlockSpec((1, d), fixed),
            pl.BlockSpec((1, d), fixed),
        ],
        out_specs=pl.BlockSpec((FFN_TM, d), lambda i, j: (i, 0)),
        out_shape=jax.ShapeDtypeStruct((m, d), jnp.float32),
        scratch_shapes=[
            pltpu.VMEM((FFN_TM + 2 * FFN_HALO, d), jnp.bfloat16),
            pltpu.VMEM((FFN_TM + 2 * FFN_HALO, FFN_TN), jnp.float32),
            pltpu.VMEM((FFN_TM + 2 * FFN_HALO, FFN_TN), jnp.float32),
            pltpu.VMEM((FFN_TM, d), jnp.float32),
        ],
        compiler_params=_cparams(("parallel", "arbitrary")),
        name="conv_ffn_ln",
    )(x, x, x, w_in, w_in, b_in, b_in, conv_w, conv_w, conv_b, conv_b, w_out, b_out, g, beta)


def kernel(x, attn_w_qkv, attn_w_o, attn_lambda, attn_subln_g, na_w_qkv, na_b_qkv, na_rpb,
           na_w_o, na_b_o, ffn_w_in, ffn_b_in, ffn_conv_w, ffn_conv_b, ffn_w_out, ffn_b_out,
           ln_g, ln_b):
    batch, seq, d = x.shape
    bf16 = jnp.bfloat16
    f32 = jnp.float32
    xf = x.reshape(batch * seq, d)
    q_scale = HEAD_DIM ** -0.5

    n_attn_heads = attn_w_qkv.shape[2] // (3 * 2 * HEAD_DIM)
    slopes = jnp.exp2(-8.0 * jnp.arange(1, n_attn_heads + 1, dtype=f32) / n_attn_heads)
    qk_cols = 2 * n_attn_heads * HEAD_DIM
    attn_scale = jnp.where(jnp.arange(attn_w_qkv.shape[2]) < qk_cols, q_scale, 1.0).astype(f32)[None]
    na_dim = na_w_qkv.shape[2] // 3
    na_scale = jnp.where(jnp.arange(3 * na_dim) < na_dim, q_scale, 1.0).astype(f32)[None]
    zeros_d = jnp.zeros((1, d), f32)
    qf, kf = _alibi_features(n_attn_heads, seq)
    diag = _diag_correction()

    for i in range(DEPTH):
        j = i // 2
        if i % 2 == 0:
            lambda_init = 0.8 - 0.6 * math.exp(-0.3 * i)
            w = attn_w_qkv[j].astype(bf16)
            qkv = _qkv_proj(xf, w, jnp.zeros((1, w.shape[1]), f32), attn_scale)
            o = _diff_attention(qkv, slopes, qf, kf, diag, attn_lambda[j], attn_subln_g[j][None],
                                batch, seq, lambda_init)
            w_o, b_o = attn_w_o[j].astype(bf16), zeros_d
        else:
            qkv = _qkv_proj(xf, na_w_qkv[j].astype(bf16), na_b_qkv[j][None], na_scale)
            rpb_pad = jnp.pad(na_rpb[j], ((0, 0), (0, 0), (0, LANES - na_rpb.shape[3])))
            o = _neighborhood_attention(qkv, rpb_pad, batch, seq)
            w_o, b_o = na_w_o[j].astype(bf16), na_b_o[j][None]
        xf = _out_proj_ln(o, xf, w_o, b_o, ln_g[i, 0][None], ln_b[i, 0][None])
        xf = _conv_ffn_ln(xf, seq, ffn_w_in[i].astype(bf16), ffn_b_in[i][None], ffn_conv_w[i],
                          ffn_conv_b[i][None], ffn_w_out[i].astype(bf16), ffn_b_out[i][None],
                          ln_g[i, 1][None], ln_b[i, 1][None])
    return xf.reshape(batch, seq, d)
```

```python
import functools
import math

import jax
import jax.numpy as jnp
from jax import lax
from jax.experimental import pallas as pl
from jax.experimental.pallas import tpu as pltpu

DEPTH = 4
HEAD_DIM = 64
GRID_W = 64
NA_KH = 8
NA_KW = 16
LN_EPS = 1e-5
RMS_EPS = 1e-5
ALPHA = (2.0 * DEPTH) ** 0.25

LANES = 128
VMEM_LIMIT_BYTES = 56 * 1024 * 1024

NEG_BIG = -1e30

PROJ_TM = 1024
PROJ_TN = 1024
OUT_TM = 512
FFN_TM = 1024
FFN_TN = 256
FFN_HALO = 16
FFN_RB = 256
ATT_TQ = 512
ATT_TK = 1024
ATT_UNROLL = 4
POS_SPLIT = 256
NA_QROWS = 4
NA_KROWS = NA_QROWS + NA_KH
NA_TILES_PER_STEP = 4


def _cparams(sem):
    return pltpu.CompilerParams(dimension_semantics=sem, vmem_limit_bytes=VMEM_LIMIT_BYTES)


def _layer_norm_rows(z, g, b):
    mu = jnp.mean(z, axis=-1, keepdims=True)
    zc = z - mu
    var = jnp.mean(zc * zc, axis=-1, keepdims=True)
    return zc * lax.rsqrt(var + LN_EPS) * g + b


def _proj_kernel(x_ref, w_ref, b_ref, s_ref, o_ref):
    acc = jnp.dot(x_ref[...].astype(jnp.bfloat16), w_ref[...], preferred_element_type=jnp.float32)
    o_ref[...] = ((acc + b_ref[...]) * s_ref[...]).astype(o_ref.dtype)


def _qkv_proj(x, w, bias, colscale):
    m, d = x.shape
    n = w.shape[1]
    return pl.pallas_call(
        _proj_kernel,
        grid=(m // PROJ_TM, n // PROJ_TN),
        in_specs=[
            pl.BlockSpec((PROJ_TM, d), lambda i, j: (i, 0)),
            pl.BlockSpec((d, PROJ_TN), lambda i, j: (0, j)),
            pl.BlockSpec((1, PROJ_TN), lambda i, j: (0, j)),
            pl.BlockSpec((1, PROJ_TN), lambda i, j: (0, j)),
        ],
        out_specs=pl.BlockSpec((PROJ_TM, PROJ_TN), lambda i, j: (i, j)),
        out_shape=jax.ShapeDtypeStruct((m, n), jnp.bfloat16),
        compiler_params=_cparams(("parallel", "arbitrary")),
        name="qkv_proj",
    )(x, w, bias, colscale)


def _out_ln_kernel(o_ref, x_ref, w_ref, b_ref, g_ref, beta_ref, y_ref):
    mix = jnp.dot(o_ref[...], w_ref[...], preferred_element_type=jnp.float32) + b_ref[...]
    z = ALPHA * x_ref[...] + mix
    y_ref[...] = _layer_norm_rows(z, g_ref[...], beta_ref[...])


def _out_proj_ln(o, x, w, b, g, beta):
    m, d = x.shape
    k = o.shape[1]
    row = lambda i: (i, 0)
    fixed = lambda i: (0, 0)
    return pl.pallas_call(
        _out_ln_kernel,
        grid=(m // OUT_TM,),
        in_specs=[
            pl.BlockSpec((OUT_TM, k), row),
            pl.BlockSpec((OUT_TM, d), row),
            pl.BlockSpec((k, d), fixed),
            pl.BlockSpec((1, d), fixed),
            pl.BlockSpec((1, d), fixed),
            pl.BlockSpec((1, d), fixed),
        ],
        out_specs=pl.BlockSpec((OUT_TM, d), row),
        out_shape=jax.ShapeDtypeStruct((m, d), jnp.float32),
        compiler_params=_cparams(("parallel",)),
        name="out_proj_ln",
    )(o, x, w, b, g, beta)


def _alibi_features(n_heads, seq):
    assert 8 % n_heads == 0 and seq <= POS_SPLIT * POS_SPLIT
    m = jnp.exp2(-8.0 * jnp.arange(1, n_heads + 1, dtype=jnp.float32) / n_heads)[:, None]
    pos = jnp.arange(seq, dtype=jnp.int32)
    hi = ((pos // POS_SPLIT) * POS_SPLIT).astype(jnp.float32)[None]
    lo = (pos % POS_SPLIT).astype(jnp.float32)[None]
    one = jnp.ones((n_heads, seq), jnp.float32)
    pad = jnp.zeros((n_heads, seq, LANES - 4), jnp.float32)
    qf = jnp.concatenate([jnp.stack([-m * hi, -m * lo, one, one], axis=-1), pad], axis=-1)
    kf = jnp.concatenate([jnp.stack([one, one, m * hi, m * lo], axis=-1), pad], axis=-1)
    return qf.astype(jnp.bfloat16), kf.astype(jnp.bfloat16)


def _diag_correction():
    t = jnp.arange(ATT_TQ, dtype=jnp.float32)[None, :, None]
    s = jnp.arange(ATT_TK, dtype=jnp.float32)[None, None, :]
    off = (jnp.arange(ATT_TK // ATT_TQ, dtype=jnp.float32) * ATT_TQ)[:, None, None]
    return 2.0 * jnp.maximum(s - t - off, 0.0)


def _diff_attn_kernel(slopes_ref, q_ref, qf_ref, k_ref, kf_ref, v_ref, diag_ref, lam_ref, g_ref,
                      o_ref, ka_scr, va_scr, qa_scr, s0_scr, s1_scr, mrun_scr, mfin_scr, acc_scr,
                      *, lambda_init, n_chunks, n_tiles):
    h = pl.program_id(1)
    i = pl.program_id(2)
    tq = q_ref.shape[0]
    diag_chunk = i // (ATT_TK // ATT_TQ)
    blocks = ATT_TK // LANES
    s_scrs = (s0_scr, s1_scr)

    @pl.when(i == 0)
    def _():
        ka_scr[:, :LANES] = k_ref[...]
        ka_scr[:, LANES:] = kf_ref[...]
        va_scr[:, :LANES] = v_ref[...]
        va_scr[:, LANES:] = jnp.ones(v_ref.shape, v_ref.dtype)
        acc_scr[...] = jnp.zeros_like(acc_scr)

    @pl.when(i < n_tiles)
    def _():
        q = q_ref[...]
        qf = qf_ref[...]
        lane = lax.broadcasted_iota(jnp.int32, q.shape, 1)
        zero = jnp.zeros_like(q)
        for c in range(2):
            qc = jnp.where((lane >= c * HEAD_DIM) & (lane < (c + 1) * HEAD_DIM), q, zero)
            qa_scr[2 * c] = jnp.concatenate([qc, qf], axis=1)
            qa_scr[2 * c + 1] = jnp.concatenate([qc, -qf], axis=1)
        mrun_scr[...] = jnp.full(mrun_scr.shape, -jnp.inf, jnp.float32)

    def score_one(c, j):
        start = pl.multiple_of(j * ATT_TK, ATT_TK)
        after = (j > diag_chunk).astype(jnp.int32)
        s = lax.dot_general(qa_scr[2 * c + after], ka_scr[pl.ds(start, ATT_TK), :],
                            (((1,), (1,)), ((), ())), preferred_element_type=jnp.float32)
        coef = jnp.where(j == diag_chunk, slopes_ref[h], 0.0)
        s = s - coef * diag_ref[...]
        s_scrs[c][j] = s
        mx = s[:, :LANES]
        for kb in range(1, blocks):
            mx = jnp.maximum(mx, s[:, kb * LANES:(kb + 1) * LANES])
        mrun_scr[c] = jnp.maximum(mrun_scr[c], mx)

    def pv_one(c, j):
        start = pl.multiple_of(j * ATT_TK, ATT_TK)
        s = s_scrs[c][j]
        mb = mfin_scr[c]
        p = jnp.concatenate(
            [jnp.exp(s[:, kb * LANES:(kb + 1) * LANES] - mb).astype(jnp.bfloat16) for kb in range(blocks)],
            axis=1)
        acc_scr[c] += jnp.dot(p, va_scr[pl.ds(start, ATT_TK), :], preferred_element_type=jnp.float32)

    def run_chunks(score_c, pv_c):
        def body(jj, carry):
            for u in range(ATT_UNROLL):
                j = jj * ATT_UNROLL + u
                if score_c is not None:
                    score_one(score_c, j)
                if pv_c is not None:
                    pv_one(pv_c, j)
            return carry
        lax.fori_loop(0, n_chunks // ATT_UNROLL, body, 0)

    def finish_max(c):
        mfin_scr[c] = jnp.broadcast_to(jnp.max(mrun_scr[c], axis=-1, keepdims=True), (tq, LANES))

    @pl.when(i == 0)
    def _():
        run_chunks(0, None)

    @pl.when((i > 0) & (i < n_tiles))
    def _():
        run_chunks(0, 1)

    @pl.when(i == n_tiles)
    def _():
        run_chunks(None, 1)

    @pl.when(i > 0)
    def _():
        lf = lam_ref[...]
        lam = (jnp.exp(jnp.sum(lf[0:1, :] * lf[1:2, :], axis=-1, keepdims=True))
               - jnp.exp(jnp.sum(lf[2:3, :] * lf[3:4, :], axis=-1, keepdims=True)) + lambda_init)
        a1 = acc_scr[0]
        a2 = acc_scr[1]
        o = a1[:, :LANES] / a1[:, LANES:] - lam * (a2[:, :LANES] / a2[:, LANES:])
        y = o * lax.rsqrt(jnp.mean(o * o, axis=-1, keepdims=True) + RMS_EPS) * g_ref[...]
        o_ref[...] = (y * (1.0 - lambda_init)).astype(o_ref.dtype)
        acc_scr[...] = jnp.zeros_like(acc_scr)

    @pl.when(i < n_tiles)
    def _():
        finish_max(0)
        run_chunks(1, 0)
        finish_max(1)


def _diff_attention(qkv, slopes, qf, kf, diag, lam_vec, subln_g, batch, seq, lambda_init):
    n_heads = qkv.shape[1] // (3 * LANES)
    assert ATT_TK % ATT_TQ == 0 and seq % (ATT_TK * ATT_UNROLL) == 0
    nq = seq // ATT_TQ
    n_chunks = seq // ATT_TK
    tiles_per_chunk = ATT_TK // ATT_TQ
    kern = functools.partial(_diff_attn_kernel, lambda_init=lambda_init, n_chunks=n_chunks, n_tiles=nq)
    fixed = lambda b, h, i: (0, 0)
    scored = lambda i: jnp.minimum(i, nq - 1)
    finished = lambda i: jnp.maximum(i - 1, 0)
    return pl.pallas_call(
        kern,
        grid=(batch, n_heads, nq + 1),
        in_specs=[
            pl.BlockSpec(memory_space=pltpu.SMEM),
            pl.BlockSpec((ATT_TQ, LANES), lambda b, h, i: (b * nq + scored(i), h)),
            pl.BlockSpec((None, ATT_TQ, LANES), lambda b, h, i: (h, scored(i), 0)),
            pl.BlockSpec((seq, LANES), lambda b, h, i: (b, n_heads + h)),
            pl.BlockSpec((None, seq, LANES), lambda b, h, i: (h, 0, 0)),
            pl.BlockSpec((seq, LANES), lambda b, h, i: (b, 2 * n_heads + h)),
            pl.BlockSpec((None, ATT_TQ, ATT_TK), lambda b, h, i: (scored(i) % tiles_per_chunk, 0, 0)),
            pl.BlockSpec((4, HEAD_DIM), fixed),
            pl.BlockSpec((1, LANES), fixed),
        ],
        out_specs=pl.BlockSpec((ATT_TQ, LANES), lambda b, h, i: (b * nq + finished(i), h)),
        out_shape=jax.ShapeDtypeStruct((batch * seq, n_heads * LANES), jnp.bfloat16),
        scratch_shapes=[
            pltpu.VMEM((seq, 2 * LANES), jnp.bfloat16),
            pltpu.VMEM((seq, 2 * LANES), jnp.bfloat16),
            pltpu.VMEM((4, ATT_TQ, 2 * LANES), jnp.bfloat16),
            pltpu.VMEM((n_chunks, ATT_TQ, ATT_TK), jnp.float32),
            pltpu.VMEM((n_chunks, ATT_TQ, ATT_TK), jnp.float32),
            pltpu.VMEM((2, ATT_TQ, LANES), jnp.float32),
            pltpu.VMEM((2, ATT_TQ, LANES), jnp.float32),
            pltpu.VMEM((2, ATT_TQ, 2 * LANES), jnp.float32),
        ],
        compiler_params=_cparams(("parallel", "parallel", "arbitrary")),
        name="diff_attn",
    )(slopes, qkv, qf, qkv, kf, qkv, diag, lam_vec, subln_g)


def _na_row_offset(cls, qr, kr):
    if cls == 0:
        return kr - qr + NA_KH - 1 if kr < NA_KH else None
    if cls == 2:
        return kr - qr - 1 if kr >= NA_KROWS - NA_KH else None
    return kr - qr + NA_KH // 2 - 1 if 0 <= kr - qr < NA_KH else None


def _na_build_bias(rpb_ref, bias_scr):
    shape = (GRID_W, LANES)
    qc = lax.broadcasted_iota(jnp.int32, shape, 0)
    lane = lax.broadcasted_iota(jnp.int32, shape, 1)
    kc = lane % GRID_W
    col_start = jnp.clip(qc - NA_KW // 2, 0, GRID_W - NA_KW)
    in_cols = (kc >= col_start) & (kc < col_start + NA_KW)
    left = lane < GRID_W
    neg = jnp.full(shape, NEG_BIG, jnp.float32)

    def toeplitz(hh, ro, lane_off):
        vec = jnp.broadcast_to(rpb_ref[hh, ro:ro + 1, :], shape)
        shift = (lane_off - (NA_KW - 1)) % LANES
        return pltpu.roll(vec, shift, 1, stride=1, stride_axis=0)

    for cls in range(3):
        for hh in range(2):
            for qr in range(NA_QROWS):
                for kp in range(NA_KROWS // 2):
                    ro_a = _na_row_offset(cls, qr, 2 * kp)
                    ro_b = _na_row_offset(cls, qr, 2 * kp + 1)
                    blk_a = neg if ro_a is None else jnp.where(in_cols, toeplitz(hh, ro_a, 0), neg)
                    blk_b = neg if ro_b is None else jnp.where(in_cols, toeplitz(hh, ro_b, GRID_W), neg)
                    bias_scr[cls, hh, qr * GRID_W:(qr + 1) * GRID_W, kp * LANES:(kp + 1) * LANES] = (
                        jnp.where(left, blk_a, blk_b))


def _na_kernel(q_ref, k_ref, v_ref, rpb_ref, o_ref, bias_scr, *, n_tiles):
    b = pl.program_id(1)
    t = pl.program_id(2)

    @pl.when((b == 0) & (t == 0))
    def _():
        _na_build_bias(rpb_ref, bias_scr)

    n_keys = NA_KROWS * GRID_W
    tq = NA_QROWS * GRID_W
    rows_total = n_tiles * NA_QROWS
    lane = lax.broadcasted_iota(jnp.int32, (tq, LANES), 1)
    ones = jnp.ones((n_keys, LANES), jnp.bfloat16)

    for sub in range(NA_TILES_PER_STEP):
        tile = t * NA_TILES_PER_STEP + sub
        ws = jnp.clip(tile * NA_QROWS - NA_KH // 2, 0, rows_total - NA_KROWS)
        start = pl.multiple_of(ws * GRID_W, GRID_W)
        kw = k_ref[pl.ds(start, n_keys), :]
        vw = jnp.concatenate([v_ref[pl.ds(start, n_keys), :], ones], axis=1)
        cls = jnp.where(tile == 0, 0, jnp.where(tile == n_tiles - 1, 2, 1))

        q = q_ref[sub * tq:(sub + 1) * tq, :]
        zero = jnp.zeros_like(q)
        outs = []
        for hh in range(2):
            qm = jnp.where((lane >= hh * HEAD_DIM) & (lane < (hh + 1) * HEAD_DIM), q, zero)
            s = lax.dot_general(qm, kw, (((1,), (1,)), ((), ())), preferred_element_type=jnp.float32)
            s = s + bias_scr[cls, hh]
            p = jnp.exp(s - jnp.max(s, axis=-1, keepdims=True))
            pv = jnp.dot(p.astype(jnp.bfloat16), vw, preferred_element_type=jnp.float32)
            outs.append(pv[:, :LANES] / pv[:, LANES:])
        o_ref[sub * tq:(sub + 1) * tq, :] = jnp.where(lane < HEAD_DIM, outs[0], outs[1]).astype(o_ref.dtype)


def _neighborhood_attention(qkv, rpb_pad, batch, seq):
    n_pairs = qkv.shape[1] // (3 * LANES)
    tq = NA_QROWS * GRID_W
    n_tiles = seq // tq
    n_steps = n_tiles // NA_TILES_PER_STEP
    blk = NA_TILES_PER_STEP * tq
    assert seq // GRID_W >= NA_KROWS and seq % blk == 0
    kern = functools.partial(_na_kernel, n_tiles=n_tiles)
    return pl.pallas_call(
        kern,
        grid=(n_pairs, batch, n_steps),
        in_specs=[
            pl.BlockSpec((blk, LANES), lambda hp, b, t: (b * n_steps + t, hp)),
            pl.BlockSpec((seq, LANES), lambda hp, b, t: (b, n_pairs + hp)),
            pl.BlockSpec((seq, LANES), lambda hp, b, t: (b, 2 * n_pairs + hp)),
            pl.BlockSpec((2, 2 * NA_KH - 1, LANES), lambda hp, b, t: (hp, 0, 0)),
        ],
        out_specs=pl.BlockSpec((blk, LANES), lambda hp, b, t: (b * n_steps + t, hp)),
        out_shape=jax.ShapeDtypeStruct((batch * seq, n_pairs * LANES), jnp.bfloat16),
        scratch_shapes=[pltpu.VMEM((3, 2, tq, NA_KROWS * GRID_W), jnp.float32)],
        compiler_params=_cparams(("arbitrary", "arbitrary", "arbitrary")),
        name="na_attn",
    )(qkv, qkv, qkv, rpb_pad)


def _ffn_kernel(x_ref, xp_ref, xn_ref, w_ref, p_ref, wo_ref, bo_ref, g_ref, beta_ref, y_ref,
                xb_scr, hu_scr, hg_scr, acc_scr, *, tiles_per_seq):
    i = pl.program_id(0)
    j = pl.program_id(1)
    tm = x_ref.shape[0]
    tn = FFN_TN
    halo = FFN_HALO

    @pl.when(j == 0)
    def _():
        xb_scr[:halo, :] = xp_ref[...].astype(jnp.bfloat16)
        xb_scr[halo:halo + tm, :] = x_ref[...].astype(jnp.bfloat16)
        xb_scr[halo + tm:, :] = xn_ref[...].astype(jnp.bfloat16)
        acc_scr[...] = jnp.zeros_like(acc_scr)

    has_prev = (i % tiles_per_seq != 0).astype(jnp.float32)
    has_next = (i % tiles_per_seq != tiles_per_seq - 1).astype(jnp.float32)
    n_sub = tm // FFN_RB

    def up_proj(r, half, h_scr):
        lo = 0 if r == 0 else r * FFN_RB + 2 * halo
        hi = (r + 1) * FFN_RB + 2 * halo
        cols = slice(half * tn, (half + 1) * tn)
        h = (jnp.dot(xb_scr[lo:hi, :], w_ref[:, cols], preferred_element_type=jnp.float32)
             + p_ref[0:1, cols])
        if r == 0:
            h_scr[:halo, :] = h[:halo] * has_prev
            h_scr[halo:hi, :] = h[halo:]
        elif r == n_sub - 1:
            h_scr[lo:hi - halo, :] = h[:hi - halo - lo]
            h_scr[hi - halo:hi, :] = h[hi - halo - lo:] * has_next
        else:
            h_scr[lo:hi, :] = h

    def conv(r, half, h_scr):
        cols = slice(half * tn, (half + 1) * tn)
        base = r * FFN_RB + halo
        return (h_scr[pl.ds(base - 1, FFN_RB), :] * p_ref[1:2, cols]
                + h_scr[pl.ds(base, FFN_RB), :] * p_ref[2:3, cols]
                + h_scr[pl.ds(base + 1, FFN_RB), :] * p_ref[3:4, cols] + p_ref[4:5, cols])

    up_proj(0, 0, hu_scr)
    up_proj(0, 1, hg_scr)
    for r in range(n_sub):
        if r + 1 < n_sub:
            up_proj(r + 1, 0, hu_scr)
            up_proj(r + 1, 1, hg_scr)
        u = conv(r, 0, hu_scr)
        g = conv(r, 1, hg_scr)
        gelu = 0.5 * g * (1.0 + lax.erf(g * (1.0 / math.sqrt(2.0))))
        a = (u * gelu).astype(jnp.bfloat16)
        acc_scr[r * FFN_RB:(r + 1) * FFN_RB, :] += jnp.dot(a, wo_ref[...],
                                                         preferred_element_type=jnp.float32)

    @pl.when(j == pl.num_programs(1) - 1)
    def _():
        z = ALPHA * x_ref[...] + acc_scr[...] + bo_ref[...]
        y_ref[...] = _layer_norm_rows(z, g_ref[...], beta_ref[...])


def _ffn_blocked_params(w_in, b_in, conv_w, conv_b):
    d, two_ff = w_in.shape
    d_ff = two_ff // 2
    nj = d_ff // FFN_TN
    assert d_ff % FFN_TN == 0

    def blocked(a):
        rows = a.shape[0]
        a = a.reshape(rows, 2, nj, FFN_TN)
        return jnp.transpose(a, (2, 0, 1, 3)).reshape(nj, rows, 2 * FFN_TN)

    w = blocked(w_in).astype(jnp.bfloat16)
    small = jnp.concatenate([b_in[None], conv_w, conv_b[None], jnp.zeros((3, two_ff), w_in.dtype)], axis=0)
    return w, blocked(small)


def _conv_ffn_ln(x, seq, w_blk, p_blk, w_out, b_out, g, beta):
    m, d = x.shape
    nj = w_blk.shape[0]
    n_i = m // FFN_TM
    tiles_per_seq = seq // FFN_TM
    assert seq % FFN_TM == 0 and FFN_TM % FFN_RB == 0 and FFN_TM // FFN_RB >= 2
    assert w_out.shape[0] == nj * FFN_TN
    halo_per_tile = FFN_TM // FFN_HALO
    n_halo_blocks = m // FFN_HALO
    fixed = lambda i, j: (0, 0)
    kern = functools.partial(_ffn_kernel, tiles_per_seq=tiles_per_seq)
    return pl.pallas_call(
        kern,
        grid=(n_i, nj),
        in_specs=[
            pl.BlockSpec((FFN_TM, d), lambda i, j: (i, 0)),
            pl.BlockSpec((FFN_HALO, d), lambda i, j: (jnp.maximum(i * halo_per_tile - 1, 0), 0)),
            pl.BlockSpec((FFN_HALO, d),
                         lambda i, j: (jnp.minimum((i + 1) * halo_per_tile, n_halo_blocks - 1), 0)),
            pl.BlockSpec((None, d, 2 * FFN_TN), lambda i, j: (j, 0, 0)),
            pl.BlockSpec((None, 8, 2 * FFN_TN), lambda i, j: (j, 0, 0)),
            pl.BlockSpec((FFN_TN, d), lambda i, j: (j, 0)),
            pl.BlockSpec((1, d), fixed),
            pl.BlockSpec((1, d), fixed),
            pl.BlockSpec((1, d), fixed),
        ],
        out_specs=pl.BlockSpec((FFN_TM, d), lambda i, j: (i, 0)),
        out_shape=jax.ShapeDtypeStruct((m, d), jnp.float32),
        scratch_shapes=[
            pltpu.VMEM((FFN_TM + 2 * FFN_HALO, d), jnp.bfloat16),
            pltpu.VMEM((FFN_TM + 2 * FFN_HALO, FFN_TN), jnp.float32),
            pltpu.VMEM((FFN_TM + 2 * FFN_HALO, FFN_TN), jnp.float32),
            pltpu.VMEM((FFN_TM, d), jnp.float32),
        ],
        compiler_params=_cparams(("parallel", "arbitrary")),
        name="conv_ffn_ln",
    )(x, x, x, w_blk, p_blk, w_out, b_out, g, beta)


def kernel(x, attn_w_qkv, attn_w_o, attn_lambda, attn_subln_g, na_w_qkv, na_b_qkv, na_rpb,
           na_w_o, na_b_o, ffn_w_in, ffn_b_in, ffn_conv_w, ffn_conv_b, ffn_w_out, ffn_b_out,
           ln_g, ln_b):
    batch, seq, d = x.shape
    bf16 = jnp.bfloat16
    f32 = jnp.float32
    xf = x.reshape(batch * seq, d)
    q_scale = HEAD_DIM ** -0.5

    n_attn_heads = attn_w_qkv.shape[2] // (3 * 2 * HEAD_DIM)
    slopes = jnp.exp2(-8.0 * jnp.arange(1, n_attn_heads + 1, dtype=f32) / n_attn_heads)
    qk_cols = 2 * n_attn_heads * HEAD_DIM
    attn_scale = jnp.where(jnp.arange(attn_w_qkv.shape[2]) < qk_cols, q_scale, 1.0).astype(f32)[None]
    na_dim = na_w_qkv.shape[2] // 3
    na_scale = jnp.where(jnp.arange(3 * na_dim) < na_dim, q_scale, 1.0).astype(f32)[None]
    zeros_d = jnp.zeros((1, d), f32)
    qf, kf = _alibi_features(n_attn_heads, seq)
    diag = _diag_correction()

    for i in range(DEPTH):
        j = i // 2
        if i % 2 == 0:
            lambda_init = 0.8 - 0.6 * math.exp(-0.3 * i)
            w = attn_w_qkv[j].astype(bf16)
            qkv = _qkv_proj(xf, w, jnp.zeros((1, w.shape[1]), f32), attn_scale)
            o = _diff_attention(qkv, slopes, qf, kf, diag, attn_lambda[j], attn_subln_g[j][None],
                                batch, seq, lambda_init)
            w_o, b_o = attn_w_o[j].astype(bf16), zeros_d
        else:
            qkv = _qkv_proj(xf, na_w_qkv[j].astype(bf16), na_b_qkv[j][None], na_scale)
            rpb_pad = jnp.pad(na_rpb[j], ((0, 0), (0, 0), (0, LANES - na_rpb.shape[3])))
            o = _neighborhood_attention(qkv, rpb_pad, batch, seq)
            w_o, b_o = na_w_o[j].astype(bf16), na_b_o[j][None]
        xf = _out_proj_ln(o, xf, w_o, b_o, ln_g[i, 0][None], ln_b[i, 0][None])
        w_blk, p_blk = _ffn_blocked_params(ffn_w_in[i], ffn_b_in[i], ffn_conv_w[i], ffn_conv_b[i])
        xf = _conv_ffn_ln(xf, seq, w_blk, p_blk, ffn_w_out[i].astype(bf16), ffn_b_out[i][None],
                          ln_g[i, 1][None], ln_b[i, 1][None])
    return xf.reshape(batch, seq, d)
```

```python
import functools
import math

import jax
import jax.numpy as jnp
from jax import lax
from jax.experimental import pallas as pl
from jax.experimental.pallas import tpu as pltpu

DEPTH = 4
HEAD_DIM = 64
GRID_W = 64
NA_KH = 8
NA_KW = 16
LN_EPS = 1e-5
RMS_EPS = 1e-5
ALPHA = (2.0 * DEPTH) ** 0.25

LANES = 128
VMEM_LIMIT_BYTES = 56 * 1024 * 1024

NEG_BIG = -1e30

PROJ_TM = 1024
PROJ_TN = 3072
OUT_TM = 512
FFN_TM = 1024
FFN_TN = 256
FFN_HALO = 16
FFN_RC = 256
ATT_TQ = 512
ATT_TK = 1024
ATT_UNROLL = 4
POS_SPLIT = 256
NA_QROWS = 4
NA_KROWS = NA_QROWS + NA_KH
NA_TILES_PER_STEP = 4


def _cparams(sem):
    return pltpu.CompilerParams(dimension_semantics=sem, vmem_limit_bytes=VMEM_LIMIT_BYTES)


def _layer_norm_rows(z, g, b):
    mu = jnp.mean(z, axis=-1, keepdims=True)
    zc = z - mu
    var = jnp.mean(zc * zc, axis=-1, keepdims=True)
    return zc * lax.rsqrt(var + LN_EPS) * g + b


def _proj_kernel(x_ref, w_ref, b_ref, s_ref, o_ref):
    acc = jnp.dot(x_ref[...].astype(jnp.bfloat16), w_ref[...], preferred_element_type=jnp.float32)
    o_ref[...] = ((acc + b_ref[...]) * s_ref[...]).astype(o_ref.dtype)


def _qkv_proj(x, w, bias, colscale):
    m, d = x.shape
    n = w.shape[1]
    return pl.pallas_call(
        _proj_kernel,
        grid=(m // PROJ_TM, n // PROJ_TN),
        in_specs=[
            pl.BlockSpec((PROJ_TM, d), lambda i, j: (i, 0)),
            pl.BlockSpec((d, PROJ_TN), lambda i, j: (0, j)),
            pl.BlockSpec((1, PROJ_TN), lambda i, j: (0, j)),
            pl.BlockSpec((1, PROJ_TN), lambda i, j: (0, j)),
        ],
        out_specs=pl.BlockSpec((PROJ_TM, PROJ_TN), lambda i, j: (i, j)),
        out_shape=jax.ShapeDtypeStruct((m, n), jnp.bfloat16),
        compiler_params=_cparams(("parallel", "arbitrary")),
        name="qkv_proj",
    )(x, w, bias, colscale)


def _out_ln_kernel(o_ref, x_ref, w_ref, b_ref, g_ref, beta_ref, y_ref):
    mix = jnp.dot(o_ref[...], w_ref[...], preferred_element_type=jnp.float32) + b_ref[...]
    z = ALPHA * x_ref[...] + mix
    y_ref[...] = _layer_norm_rows(z, g_ref[...], beta_ref[...])


def _out_proj_ln(o, x, w, b, g, beta):
    m, d = x.shape
    k = o.shape[1]
    row = lambda i: (i, 0)
    fixed = lambda i: (0, 0)
    return pl.pallas_call(
        _out_ln_kernel,
        grid=(m // OUT_TM,),
        in_specs=[
            pl.BlockSpec((OUT_TM, k), row),
            pl.BlockSpec((OUT_TM, d), row),
            pl.BlockSpec((k, d), fixed),
            pl.BlockSpec((1, d), fixed),
            pl.BlockSpec((1, d), fixed),
            pl.BlockSpec((1, d), fixed),
        ],
        out_specs=pl.BlockSpec((OUT_TM, d), row),
        out_shape=jax.ShapeDtypeStruct((m, d), jnp.float32),
        compiler_params=_cparams(("parallel",)),
        name="out_proj_ln",
    )(o, x, w, b, g, beta)


def _alibi_features(n_heads, seq):
    assert 8 % n_heads == 0 and seq <= POS_SPLIT * POS_SPLIT
    m = jnp.exp2(-8.0 * jnp.arange(1, n_heads + 1, dtype=jnp.float32) / n_heads)[:, None]
    pos = jnp.arange(seq, dtype=jnp.int32)
    hi = ((pos // POS_SPLIT) * POS_SPLIT).astype(jnp.float32)[None]
    lo = (pos % POS_SPLIT).astype(jnp.float32)[None]
    one = jnp.ones((n_heads, seq), jnp.float32)
    pad = jnp.zeros((n_heads, seq, LANES - 4), jnp.float32)
    qf = jnp.concatenate([jnp.stack([-m * hi, -m * lo, one, one], axis=-1), pad], axis=-1)
    kf = jnp.concatenate([jnp.stack([one, one, m * hi, m * lo], axis=-1), pad], axis=-1)
    return qf.astype(jnp.bfloat16), kf.astype(jnp.bfloat16)


def _diag_correction():
    t = jnp.arange(ATT_TQ, dtype=jnp.float32)[None, :, None]
    s = jnp.arange(ATT_TK, dtype=jnp.float32)[None, None, :]
    off = (jnp.arange(ATT_TK // ATT_TQ, dtype=jnp.float32) * ATT_TQ)[:, None, None]
    return 2.0 * jnp.maximum(s - t - off, 0.0)


def _diff_attn_kernel(slopes_ref, q_ref, qf_ref, k_ref, kf_ref, v_ref, diag_ref, lam_ref, g_ref,
                      o_ref, ka_scr, va_scr, qa_scr, s0_scr, s1_scr, mrun_scr, mfin_scr, acc_scr,
                      *, lambda_init, n_chunks, n_tiles):
    h = pl.program_id(1)
    i = pl.program_id(2)
    tq = q_ref.shape[0]
    diag_chunk = i // (ATT_TK // ATT_TQ)
    blocks = ATT_TK // LANES
    s_scrs = (s0_scr, s1_scr)

    @pl.when(i == 0)
    def _():
        ka_scr[:, :LANES] = k_ref[...]
        ka_scr[:, LANES:] = kf_ref[...]
        va_scr[:, :LANES] = v_ref[...]
        va_scr[:, LANES:] = jnp.ones(v_ref.shape, v_ref.dtype)
        acc_scr[...] = jnp.zeros_like(acc_scr)

    @pl.when(i < n_tiles)
    def _():
        q = q_ref[...]
        qf = qf_ref[...]
        lane = lax.broadcasted_iota(jnp.int32, q.shape, 1)
        zero = jnp.zeros_like(q)
        for c in range(2):
            qc = jnp.where((lane >= c * HEAD_DIM) & (lane < (c + 1) * HEAD_DIM), q, zero)
            qa_scr[2 * c] = jnp.concatenate([qc, qf], axis=1)
            qa_scr[2 * c + 1] = jnp.concatenate([qc, -qf], axis=1)
        mrun_scr[...] = jnp.full(mrun_scr.shape, -jnp.inf, jnp.float32)

    def score_one(c, j):
        start = pl.multiple_of(j * ATT_TK, ATT_TK)
        after = (j > diag_chunk).astype(jnp.int32)
        s = lax.dot_general(qa_scr[2 * c + after], ka_scr[pl.ds(start, ATT_TK), :],
                            (((1,), (1,)), ((), ())), preferred_element_type=jnp.float32)
        coef = jnp.where(j == diag_chunk, slopes_ref[h], 0.0)
        s = s - coef * diag_ref[...]
        s_scrs[c][j] = s
        mx = s[:, :LANES]
        for kb in range(1, blocks):
            mx = jnp.maximum(mx, s[:, kb * LANES:(kb + 1) * LANES])
        mrun_scr[c] = jnp.maximum(mrun_scr[c], mx)

    def pv_one(c, j):
        start = pl.multiple_of(j * ATT_TK, ATT_TK)
        s = s_scrs[c][j]
        mb = mfin_scr[c]
        p = jnp.concatenate(
            [jnp.exp(s[:, kb * LANES:(kb + 1) * LANES] - mb).astype(jnp.bfloat16) for kb in range(blocks)],
            axis=1)
        acc_scr[c] += jnp.dot(p, va_scr[pl.ds(start, ATT_TK), :], preferred_element_type=jnp.float32)

    def run_chunks(score_c, pv_c):
        def body(jj, carry):
            for u in range(ATT_UNROLL):
                j = jj * ATT_UNROLL + u
                if score_c is not None:
                    score_one(score_c, j)
                if pv_c is not None:
                    pv_one(pv_c, j)
            return carry
        lax.fori_loop(0, n_chunks // ATT_UNROLL, body, 0)

    def finish_max(c):
        mfin_scr[c] = jnp.broadcast_to(jnp.max(mrun_scr[c], axis=-1, keepdims=True), (tq, LANES))

    @pl.when(i == 0)
    def _():
        run_chunks(0, None)

    @pl.when((i > 0) & (i < n_tiles))
    def _():
        run_chunks(0, 1)

    @pl.when(i == n_tiles)
    def _():
        run_chunks(None, 1)

    @pl.when(i > 0)
    def _():
        lf = lam_ref[...]
        lam = (jnp.exp(jnp.sum(lf[0:1, :] * lf[1:2, :], axis=-1, keepdims=True))
               - jnp.exp(jnp.sum(lf[2:3, :] * lf[3:4, :], axis=-1, keepdims=True)) + lambda_init)
        a1 = acc_scr[0]
        a2 = acc_scr[1]
        o = a1[:, :LANES] / a1[:, LANES:] - lam * (a2[:, :LANES] / a2[:, LANES:])
        y = o * lax.rsqrt(jnp.mean(o * o, axis=-1, keepdims=True) + RMS_EPS) * g_ref[...]
        o_ref[...] = (y * (1.0 - lambda_init)).astype(o_ref.dtype)
        acc_scr[...] = jnp.zeros_like(acc_scr)

    @pl.when(i < n_tiles)
    def _():
        finish_max(0)
        run_chunks(1, 0)
        finish_max(1)


def _diff_attention(qkv, slopes, qf, kf, diag, lam_vec, subln_g, batch, seq, lambda_init):
    n_heads = qkv.shape[1] // (3 * LANES)
    assert ATT_TK % ATT_TQ == 0 and seq % (ATT_TK * ATT_UNROLL) == 0
    nq = seq // ATT_TQ
    n_chunks = seq // ATT_TK
    tiles_per_chunk = ATT_TK // ATT_TQ
    kern = functools.partial(_diff_attn_kernel, lambda_init=lambda_init, n_chunks=n_chunks, n_tiles=nq)
    fixed = lambda b, h, i: (0, 0)
    scored = lambda i: jnp.minimum(i, nq - 1)
    finished = lambda i: jnp.maximum(i - 1, 0)
    return pl.pallas_call(
        kern,
        grid=(batch, n_heads, nq + 1),
        in_specs=[
            pl.BlockSpec(memory_space=pltpu.SMEM),
            pl.BlockSpec((ATT_TQ, LANES), lambda b, h, i: (b * nq + scored(i), h)),
            pl.BlockSpec((None, ATT_TQ, LANES), lambda b, h, i: (h, scored(i), 0)),
            pl.BlockSpec((seq, LANES), lambda b, h, i: (b, n_heads + h)),
            pl.BlockSpec((None, seq, LANES), lambda b, h, i: (h, 0, 0)),
            pl.BlockSpec((seq, LANES), lambda b, h, i: (b, 2 * n_heads + h)),
            pl.BlockSpec((None, ATT_TQ, ATT_TK), lambda b, h, i: (scored(i) % tiles_per_chunk, 0, 0)),
            pl.BlockSpec((4, HEAD_DIM), fixed),
            pl.BlockSpec((1, LANES), fixed),
        ],
        out_specs=pl.BlockSpec((ATT_TQ, LANES), lambda b, h, i: (b * nq + finished(i), h)),
        out_shape=jax.ShapeDtypeStruct((batch * seq, n_heads * LANES), jnp.bfloat16),
        scratch_shapes=[
            pltpu.VMEM((seq, 2 * LANES), jnp.bfloat16),
            pltpu.VMEM((seq, 2 * LANES), jnp.bfloat16),
            pltpu.VMEM((4, ATT_TQ, 2 * LANES), jnp.bfloat16),
            pltpu.VMEM((n_chunks, ATT_TQ, ATT_TK), jnp.float32),
            pltpu.VMEM((n_chunks, ATT_TQ, ATT_TK), jnp.float32),
            pltpu.VMEM((2, ATT_TQ, LANES), jnp.float32),
            pltpu.VMEM((2, ATT_TQ, LANES), jnp.float32),
            pltpu.VMEM((2, ATT_TQ, 2 * LANES), jnp.float32),
        ],
        compiler_params=_cparams(("parallel", "parallel", "arbitrary")),
        name="diff_attn",
    )(slopes, qkv, qf, qkv, kf, qkv, diag, lam_vec, subln_g)


def _na_row_offset(cls, qr, kr):
    if cls == 0:
        return kr - qr + NA_KH - 1 if kr < NA_KH else None
    if cls == 2:
        return kr - qr - 1 if kr >= NA_KROWS - NA_KH else None
    return kr - qr + NA_KH // 2 - 1 if 0 <= kr - qr < NA_KH else None


def _na_build_bias(rpb_ref, bias_scr):
    shape = (GRID_W, LANES)
    qc = lax.broadcasted_iota(jnp.int32, shape, 0)
    lane = lax.broadcasted_iota(jnp.int32, shape, 1)
    kc = lane % GRID_W
    col_start = jnp.clip(qc - NA_KW // 2, 0, GRID_W - NA_KW)
    in_cols = (kc >= col_start) & (kc < col_start + NA_KW)
    left = lane < GRID_W
    neg = jnp.full(shape, NEG_BIG, jnp.float32)

    def toeplitz(hh, ro, lane_off):
        vec = jnp.broadcast_to(rpb_ref[hh, ro:ro + 1, :], shape)
        shift = (lane_off - (NA_KW - 1)) % LANES
        return pltpu.roll(vec, shift, 1, stride=1, stride_axis=0)

    for cls in range(3):
        for hh in range(2):
            for qr in range(NA_QROWS):
                for kp in range(NA_KROWS // 2):
                    ro_a = _na_row_offset(cls, qr, 2 * kp)
                    ro_b = _na_row_offset(cls, qr, 2 * kp + 1)
                    blk_a = neg if ro_a is None else jnp.where(in_cols, toeplitz(hh, ro_a, 0), neg)
                    blk_b = neg if ro_b is None else jnp.where(in_cols, toeplitz(hh, ro_b, GRID_W), neg)
                    bias_scr[cls, hh, qr * GRID_W:(qr + 1) * GRID_W, kp * LANES:(kp + 1) * LANES] = (
                        jnp.where(left, blk_a, blk_b))


def _na_kernel(q_ref, k_ref, v_ref, rpb_ref, o_ref, bias_scr, *, n_tiles):
    b = pl.program_id(1)
    t = pl.program_id(2)

    @pl.when((b == 0) & (t == 0))
    def _():
        _na_build_bias(rpb_ref, bias_scr)

    n_keys = NA_KROWS * GRID_W
    tq = NA_QROWS * GRID_W
    rows_total = n_tiles * NA_QROWS
    lane = lax.broadcasted_iota(jnp.int32, (tq, LANES), 1)
    ones = jnp.ones((n_keys, LANES), jnp.bfloat16)

    for sub in range(NA_TILES_PER_STEP):
        tile = t * NA_TILES_PER_STEP + sub
        ws = jnp.clip(tile * NA_QROWS - NA_KH // 2, 0, rows_total - NA_KROWS)
        start = pl.multiple_of(ws * GRID_W, GRID_W)
        kw = k_ref[pl.ds(start, n_keys), :]
        vw = jnp.concatenate([v_ref[pl.ds(start, n_keys), :], ones], axis=1)
        cls = jnp.where(tile == 0, 0, jnp.where(tile == n_tiles - 1, 2, 1))

        q = q_ref[sub * tq:(sub + 1) * tq, :]
        zero = jnp.zeros_like(q)
        outs = []
        for hh in range(2):
            qm = jnp.where((lane >= hh * HEAD_DIM) & (lane < (hh + 1) * HEAD_DIM), q, zero)
            s = lax.dot_general(qm, kw, (((1,), (1,)), ((), ())), preferred_element_type=jnp.float32)
            s = s + bias_scr[cls, hh]
            p = jnp.exp(s - jnp.max(s, axis=-1, keepdims=True))
            pv = jnp.dot(p.astype(jnp.bfloat16), vw, preferred_element_type=jnp.float32)
            outs.append(pv[:, :LANES] / pv[:, LANES:])
        o_ref[sub * tq:(sub + 1) * tq, :] = jnp.where(lane < HEAD_DIM, outs[0], outs[1]).astype(o_ref.dtype)


def _neighborhood_attention(qkv, rpb_pad, batch, seq):
    n_pairs = qkv.shape[1] // (3 * LANES)
    tq = NA_QROWS * GRID_W
    n_tiles = seq // tq
    n_steps = n_tiles // NA_TILES_PER_STEP
    blk = NA_TILES_PER_STEP * tq
    assert seq // GRID_W >= NA_KROWS and seq % blk == 0
    kern = functools.partial(_na_kernel, n_tiles=n_tiles)
    return pl.pallas_call(
        kern,
        grid=(n_pairs, batch, n_steps),
        in_specs=[
            pl.BlockSpec((blk, LANES), lambda hp, b, t: (b * n_steps + t, hp)),
            pl.BlockSpec((seq, LANES), lambda hp, b, t: (b, n_pairs + hp)),
            pl.BlockSpec((seq, LANES), lambda hp, b, t: (b, 2 * n_pairs + hp)),
            pl.BlockSpec((2, 2 * NA_KH - 1, LANES), lambda hp, b, t: (hp, 0, 0)),
        ],
        out_specs=pl.BlockSpec((blk, LANES), lambda hp, b, t: (b * n_steps + t, hp)),
        out_shape=jax.ShapeDtypeStruct((batch * seq, n_pairs * LANES), jnp.bfloat16),
        scratch_shapes=[pltpu.VMEM((3, 2, tq, NA_KROWS * GRID_W), jnp.float32)],
        compiler_params=_cparams(("arbitrary", "arbitrary", "arbitrary")),
        name="na_attn",
    )(qkv, qkv, qkv, rpb_pad)


def _ffn_kernel(x_ref, xp_ref, xn_ref, wu_ref, wg_ref, bu_ref, bg_ref, cwu_ref, cwg_ref,
                cbu_ref, cbg_ref, wo_ref, bo_ref, g_ref, beta_ref, y_ref, xb_scr, hu_scr, hg_scr,
                acc_scr, *, tiles_per_seq):
    i = pl.program_id(0)
    j = pl.program_id(1)
    tm = x_ref.shape[0]
    halo = FFN_HALO

    @pl.when(j == 0)
    def _():
        xb_scr[:halo, :] = xp_ref[...].astype(jnp.bfloat16)
        xb_scr[halo:halo + tm, :] = x_ref[...].astype(jnp.bfloat16)
        xb_scr[halo + tm:, :] = xn_ref[...].astype(jnp.bfloat16)
        acc_scr[...] = jnp.zeros_like(acc_scr)

    has_prev = (i % tiles_per_seq != 0).astype(jnp.float32)
    has_next = (i % tiles_per_seq != tiles_per_seq - 1).astype(jnp.float32)
    n_sub = tm // FFN_RC

    def up_proj(r, w_ref, b_ref, h_scr):
        lo = 0 if r == 0 else r * FFN_RC + 2 * halo
        hi = (r + 1) * FFN_RC + 2 * halo
        h = jnp.dot(xb_scr[lo:hi, :], w_ref[...], preferred_element_type=jnp.float32) + b_ref[...]
        if r == 0:
            h_scr[:halo, :] = h[:halo] * has_prev
            h_scr[halo:hi, :] = h[halo:]
        elif r == n_sub - 1:
            h_scr[lo:hi - halo, :] = h[:hi - halo - lo]
            h_scr[hi - halo:hi, :] = h[hi - halo - lo:] * has_next
        else:
            h_scr[lo:hi, :] = h

    def conv(r, cw_ref, cb_ref, h_scr):
        cw = cw_ref[...]
        base = r * FFN_RC + halo
        return (h_scr[pl.ds(base - 1, FFN_RC), :] * cw[0:1, :] + h_scr[pl.ds(base, FFN_RC), :] * cw[1:2, :]
                + h_scr[pl.ds(base + 1, FFN_RC), :] * cw[2:3, :] + cb_ref[...])

    up_proj(0, wu_ref, bu_ref, hu_scr)
    up_proj(0, wg_ref, bg_ref, hg_scr)
    for r in range(n_sub):
        if r + 1 < n_sub:
            up_proj(r + 1, wu_ref, bu_ref, hu_scr)
            up_proj(r + 1, wg_ref, bg_ref, hg_scr)
        u = conv(r, cwu_ref, cbu_ref, hu_scr)
        g = conv(r, cwg_ref, cbg_ref, hg_scr)
        gelu = 0.5 * g * (1.0 + lax.erf(g * (1.0 / math.sqrt(2.0))))
        a = (u * gelu).astype(jnp.bfloat16)
        acc_scr[r * FFN_RC:(r + 1) * FFN_RC, :] += jnp.dot(a, wo_ref[...],
                                                         preferred_element_type=jnp.float32)

    @pl.when(j == pl.num_programs(1) - 1)
    def _():
        z = ALPHA * x_ref[...] + acc_scr[...] + bo_ref[...]
        y_ref[...] = _layer_norm_rows(z, g_ref[...], beta_ref[...])


def _conv_ffn_ln(x, seq, w_in, b_in, conv_w, conv_b, w_out, b_out, g, beta):
    m, d = x.shape
    d_ff = w_out.shape[0]
    nj = d_ff // FFN_TN
    n_i = m // FFN_TM
    tiles_per_seq = seq // FFN_TM
    assert seq % FFN_TM == 0 and FFN_TM % FFN_RC == 0 and FFN_TM // FFN_RC >= 2
    halo_per_tile = FFN_TM // FFN_HALO
    n_halo_blocks = m // FFN_HALO
    u_col = lambda i, j: (0, j)
    g_col = lambda i, j: (0, nj + j)
    fixed = lambda i, j: (0, 0)
    kern = functools.partial(_ffn_kernel, tiles_per_seq=tiles_per_seq)
    h_shape = (FFN_TM + 2 * FFN_HALO, FFN_TN)
    return pl.pallas_call(
        kern,
        grid=(n_i, nj),
        in_specs=[
            pl.BlockSpec((FFN_TM, d), lambda i, j: (i, 0)),
            pl.BlockSpec((FFN_HALO, d), lambda i, j: (jnp.maximum(i * halo_per_tile - 1, 0), 0)),
            pl.BlockSpec((FFN_HALO, d),
                         lambda i, j: (jnp.minimum((i + 1) * halo_per_tile, n_halo_blocks - 1), 0)),
            pl.BlockSpec((d, FFN_TN), u_col),
            pl.BlockSpec((d, FFN_TN), g_col),
            pl.BlockSpec((1, FFN_TN), u_col),
            pl.BlockSpec((1, FFN_TN), g_col),
            pl.BlockSpec((3, FFN_TN), u_col),
            pl.BlockSpec((3, FFN_TN), g_col),
            pl.BlockSpec((1, FFN_TN), u_col),
            pl.BlockSpec((1, FFN_TN), g_col),
            pl.BlockSpec((FFN_TN, d), lambda i, j: (j, 0)),
            pl.BlockSpec((1, d), fixed),
            pl.BlockSpec((1, d), fixed),
            pl.BlockSpec((1, d), fixed),
        ],
        out_specs=pl.BlockSpec((FFN_TM, d), lambda i, j: (i, 0)),
        out_shape=jax.ShapeDtypeStruct((m, d), jnp.float32),
        scratch_shapes=[
            pltpu.VMEM((FFN_TM + 2 * FFN_HALO, d), jnp.bfloat16),
            pltpu.VMEM(h_shape, jnp.float32),
            pltpu.VMEM(h_shape, jnp.float32),
            pltpu.VMEM((FFN_TM, d), jnp.float32),
        ],
        compiler_params=_cparams(("parallel", "arbitrary")),
        name="conv_ffn_ln",
    )(x, x, x, w_in, w_in, b_in, b_in, conv_w, conv_w, conv_b, conv_b, w_out, b_out, g, beta)


def kernel(x, attn_w_qkv, attn_w_o, attn_lambda, attn_subln_g, na_w_qkv, na_b_qkv, na_rpb,
           na_w_o, na_b_o, ffn_w_in, ffn_b_in, ffn_conv_w, ffn_conv_b, ffn_w_out, ffn_b_out,
           ln_g, ln_b):
    batch, seq, d = x.shape
    bf16 = jnp.bfloat16
    f32 = jnp.float32
    xf = x.reshape(batch * seq, d)
    q_scale = HEAD_DIM ** -0.5

    n_attn_heads = attn_w_qkv.shape[2] // (3 * 2 * HEAD_DIM)
    slopes = jnp.exp2(-8.0 * jnp.arange(1, n_attn_heads + 1, dtype=f32) / n_attn_heads)
    qk_cols = 2 * n_attn_heads * HEAD_DIM
    attn_scale = jnp.where(jnp.arange(attn_w_qkv.shape[2]) < qk_cols, q_scale, 1.0).astype(f32)[None]
    na_dim = na_w_qkv.shape[2] // 3
    na_scale = jnp.where(jnp.arange(3 * na_dim) < na_dim, q_scale, 1.0).astype(f32)[None]
    zeros_d = jnp.zeros((1, d), f32)
    qf, kf = _alibi_features(n_attn_heads, seq)
    diag = _diag_correction()

    for i in range(DEPTH):
        j = i // 2
        if i % 2 == 0:
            lambda_init = 0.8 - 0.6 * math.exp(-0.3 * i)
            w = attn_w_qkv[j].astype(bf16)
            qkv = _qkv_proj(xf, w, jnp.zeros((1, w.shape[1]), f32), attn_scale)
            o = _diff_attention(qkv, slopes, qf, kf, diag, attn_lambda[j], attn_subln_g[j][None],
                                batch, seq, lambda_init)
            w_o, b_o = attn_w_o[j].astype(bf16), zeros_d
        else:
            qkv = _qkv_proj(xf, na_w_qkv[j].astype(bf16), na_b_qkv[j][None], na_scale)
            rpb_pad = jnp.pad(na_rpb[j], ((0, 0), (0, 0), (0, LANES - na_rpb.shape[3])))
            o = _neighborhood_attention(qkv, rpb_pad, batch, seq)
            w_o, b_o = na_w_o[j].astype(bf16), na_b_o[j][None]
        xf = _out_proj_ln(o, xf, w_o, b_o, ln_g[i, 0][None], ln_b[i, 0][None])
        xf = _conv_ffn_ln(xf, seq, ffn_w_in[i].astype(bf16), ffn_b_in[i][None], ffn_conv_w[i],
                          ffn_conv_b[i][None], ffn_w_out[i].astype(bf16), ffn_b_out[i][None],
                          ln_g[i, 1][None], ln_b[i, 1][None])
    return xf.reshape(batch, seq, d)
```

```python
import functools
import math

import jax
import jax.numpy as jnp
from jax import lax
from jax.experimental import pallas as pl
from jax.experimental.pallas import tpu as pltpu

DEPTH = 4
HEAD_DIM = 64
GRID_W = 64
NA_KH = 8
NA_KW = 16
LN_EPS = 1e-5
RMS_EPS = 1e-5
ALPHA = (2.0 * DEPTH) ** 0.25

LANES = 128
VMEM_LIMIT_BYTES = 56 * 1024 * 1024

NEG_BIG = -1e30

PROJ_TM = 1024
PROJ_TN = 3072
OUT_TM = 512
FFN_TM = 1024
FFN_TN = 256
FFN_HALO = 16
FFN_RC = 512
ATT_TQ = 512
ATT_TK = 1024
ATT_UNROLL = 4
POS_SPLIT = 256
NA_QROWS = 4
NA_KROWS = NA_QROWS + NA_KH
NA_TILES_PER_STEP = 8


def _cparams(sem):
    return pltpu.CompilerParams(dimension_semantics=sem, vmem_limit_bytes=VMEM_LIMIT_BYTES)


def _layer_norm_rows(z, g, b):
    mu = jnp.mean(z, axis=-1, keepdims=True)
    zc = z - mu
    var = jnp.mean(zc * zc, axis=-1, keepdims=True)
    return zc * lax.rsqrt(var + LN_EPS) * g + b


def _proj_kernel(x_ref, w_ref, b_ref, s_ref, o_ref):
    acc = jnp.dot(x_ref[...].astype(jnp.bfloat16), w_ref[...], preferred_element_type=jnp.float32)
    o_ref[...] = ((acc + b_ref[...]) * s_ref[...]).astype(o_ref.dtype)


def _qkv_proj(x, w, bias, colscale):
    m, d = x.shape
    n = w.shape[1]
    return pl.pallas_call(
        _proj_kernel,
        grid=(m // PROJ_TM, n // PROJ_TN),
        in_specs=[
            pl.BlockSpec((PROJ_TM, d), lambda i, j: (i, 0)),
            pl.BlockSpec((d, PROJ_TN), lambda i, j: (0, j)),
            pl.BlockSpec((1, PROJ_TN), lambda i, j: (0, j)),
            pl.BlockSpec((1, PROJ_TN), lambda i, j: (0, j)),
        ],
        out_specs=pl.BlockSpec((PROJ_TM, PROJ_TN), lambda i, j: (i, j)),
        out_shape=jax.ShapeDtypeStruct((m, n), jnp.bfloat16),
        compiler_params=_cparams(("parallel", "arbitrary")),
        name="qkv_proj",
    )(x, w, bias, colscale)


def _out_ln_kernel(o_ref, x_ref, w_ref, b_ref, g_ref, beta_ref, y_ref):
    mix = jnp.dot(o_ref[...], w_ref[...], preferred_element_type=jnp.float32) + b_ref[...]
    z = ALPHA * x_ref[...] + mix
    y_ref[...] = _layer_norm_rows(z, g_ref[...], beta_ref[...])


def _out_proj_ln(o, x, w, b, g, beta):
    m, d = x.shape
    k = o.shape[1]
    row = lambda i: (i, 0)
    fixed = lambda i: (0, 0)
    return pl.pallas_call(
        _out_ln_kernel,
        grid=(m // OUT_TM,),
        in_specs=[
            pl.BlockSpec((OUT_TM, k), row),
            pl.BlockSpec((OUT_TM, d), row),
            pl.BlockSpec((k, d), fixed),
            pl.BlockSpec((1, d), fixed),
            pl.BlockSpec((1, d), fixed),
            pl.BlockSpec((1, d), fixed),
        ],
        out_specs=pl.BlockSpec((OUT_TM, d), row),
        out_shape=jax.ShapeDtypeStruct((m, d), jnp.float32),
        compiler_params=_cparams(("parallel",)),
        name="out_proj_ln",
    )(o, x, w, b, g, beta)


def _alibi_features(n_heads, seq):
    assert 8 % n_heads == 0 and seq <= POS_SPLIT * POS_SPLIT
    m = jnp.exp2(-8.0 * jnp.arange(1, n_heads + 1, dtype=jnp.float32) / n_heads)[:, None]
    pos = jnp.arange(seq, dtype=jnp.int32)
    hi = ((pos // POS_SPLIT) * POS_SPLIT).astype(jnp.float32)[None]
    lo = (pos % POS_SPLIT).astype(jnp.float32)[None]
    one = jnp.ones((n_heads, seq), jnp.float32)
    pad = jnp.zeros((n_heads, seq, LANES - 4), jnp.float32)
    qf = jnp.concatenate([jnp.stack([-m * hi, -m * lo, one, one], axis=-1), pad], axis=-1)
    kf = jnp.concatenate([jnp.stack([one, one, m * hi, m * lo], axis=-1), pad], axis=-1)
    return qf.astype(jnp.bfloat16), kf.astype(jnp.bfloat16)


def _diag_correction():
    t = jnp.arange(ATT_TQ, dtype=jnp.float32)[None, :, None]
    s = jnp.arange(ATT_TK, dtype=jnp.float32)[None, None, :]
    off = (jnp.arange(ATT_TK // ATT_TQ, dtype=jnp.float32) * ATT_TQ)[:, None, None]
    return 2.0 * jnp.maximum(s - t - off, 0.0)


def _diff_attn_kernel(slopes_ref, q_ref, qf_ref, k_ref, kf_ref, v_ref, diag_ref, lam_ref, g_ref,
                      o_ref, ka_scr, va_scr, qa_scr, s0_scr, s1_scr, mrun_scr, mfin_scr, acc_scr,
                      *, lambda_init, n_chunks, n_tiles):
    h = pl.program_id(1)
    i = pl.program_id(2)
    tq = q_ref.shape[0]
    diag_chunk = i // (ATT_TK // ATT_TQ)
    blocks = ATT_TK // LANES
    s_scrs = (s0_scr, s1_scr)

    @pl.when(i == 0)
    def _():
        ka_scr[:, :LANES] = k_ref[...]
        ka_scr[:, LANES:] = kf_ref[...]
        va_scr[:, :LANES] = v_ref[...]
        va_scr[:, LANES:] = jnp.ones(v_ref.shape, v_ref.dtype)
        acc_scr[...] = jnp.zeros_like(acc_scr)

    @pl.when(i < n_tiles)
    def _():
        q = q_ref[...]
        qf = qf_ref[...]
        lane = lax.broadcasted_iota(jnp.int32, q.shape, 1)
        zero = jnp.zeros_like(q)
        for c in range(2):
            qc = jnp.where((lane >= c * HEAD_DIM) & (lane < (c + 1) * HEAD_DIM), q, zero)
            qa_scr[2 * c] = jnp.concatenate([qc, qf], axis=1)
            qa_scr[2 * c + 1] = jnp.concatenate([qc, -qf], axis=1)
        mrun_scr[...] = jnp.full(mrun_scr.shape, -jnp.inf, jnp.float32)

    def score_one(c, j):
        start = pl.multiple_of(j * ATT_TK, ATT_TK)
        after = (j > diag_chunk).astype(jnp.int32)
        s = lax.dot_general(qa_scr[2 * c + after], ka_scr[pl.ds(start, ATT_TK), :],
                            (((1,), (1,)), ((), ())), preferred_element_type=jnp.float32)
        coef = jnp.where(j == diag_chunk, slopes_ref[h], 0.0)
        s = s - coef * diag_ref[...]
        s_scrs[c][j] = s
        mx = s[:, :LANES]
        for kb in range(1, blocks):
            mx = jnp.maximum(mx, s[:, kb * LANES:(kb + 1) * LANES])
        mrun_scr[c] = jnp.maximum(mrun_scr[c], mx)

    def pv_one(c, j):
        start = pl.multiple_of(j * ATT_TK, ATT_TK)
        s = s_scrs[c][j]
        mb = mfin_scr[c]
        p = jnp.concatenate(
            [jnp.exp(s[:, kb * LANES:(kb + 1) * LANES] - mb).astype(jnp.bfloat16) for kb in range(blocks)],
            axis=1)
        acc_scr[c] += jnp.dot(p, va_scr[pl.ds(start, ATT_TK), :], preferred_element_type=jnp.float32)

    def run_chunks(score_c, pv_c):
        def body(jj, carry):
            for u in range(ATT_UNROLL):
                j = jj * ATT_UNROLL + u
                if score_c is not None:
                    score_one(score_c, j)
                if pv_c is not None:
                    pv_one(pv_c, j)
            return carry
        lax.fori_loop(0, n_chunks // ATT_UNROLL, body, 0)

    def finish_max(c):
        mfin_scr[c] = jnp.broadcast_to(jnp.max(mrun_scr[c], axis=-1, keepdims=True), (tq, LANES))

    @pl.when(i == 0)
    def _():
        run_chunks(0, None)

    @pl.when((i > 0) & (i < n_tiles))
    def _():
        run_chunks(0, 1)

    @pl.when(i == n_tiles)
    def _():
        run_chunks(None, 1)

    @pl.when(i > 0)
    def _():
        lf = lam_ref[...]
        lam = (jnp.exp(jnp.sum(lf[0:1, :] * lf[1:2, :], axis=-1, keepdims=True))
               - jnp.exp(jnp.sum(lf[2:3, :] * lf[3:4, :], axis=-1, keepdims=True)) + lambda_init)
        a1 = acc_scr[0]
        a2 = acc_scr[1]
        o = a1[:, :LANES] / a1[:, LANES:] - lam * (a2[:, :LANES] / a2[:, LANES:])
        y = o * lax.rsqrt(jnp.mean(o * o, axis=-1, keepdims=True) + RMS_EPS) * g_ref[...]
        o_ref[...] = (y * (1.0 - lambda_init)).astype(o_ref.dtype)
        acc_scr[...] = jnp.zeros_like(acc_scr)

    @pl.when(i < n_tiles)
    def _():
        finish_max(0)
        run_chunks(1, 0)
        finish_max(1)


def _diff_attention(qkv, slopes, qf, kf, diag, lam_vec, subln_g, batch, seq, lambda_init):
    n_heads = qkv.shape[1] // (3 * LANES)
    assert ATT_TK % ATT_TQ == 0 and seq % (ATT_TK * ATT_UNROLL) == 0
    nq = seq // ATT_TQ
    n_chunks = seq // ATT_TK
    tiles_per_chunk = ATT_TK // ATT_TQ
    kern = functools.partial(_diff_attn_kernel, lambda_init=lambda_init, n_chunks=n_chunks, n_tiles=nq)
    fixed = lambda b, h, i: (0, 0)
    scored = lambda i: jnp.minimum(i, nq - 1)
    finished = lambda i: jnp.maximum(i - 1, 0)
    return pl.pallas_call(
        kern,
        grid=(batch, n_heads, nq + 1),
        in_specs=[
            pl.BlockSpec(memory_space=pltpu.SMEM),
            pl.BlockSpec((ATT_TQ, LANES), lambda b, h, i: (b * nq + scored(i), h)),
            pl.BlockSpec((None, ATT_TQ, LANES), lambda b, h, i: (h, scored(i), 0)),
            pl.BlockSpec((seq, LANES), lambda b, h, i: (b, n_heads + h)),
            pl.BlockSpec((None, seq, LANES), lambda b, h, i: (h, 0, 0)),
            pl.BlockSpec((seq, LANES), lambda b, h, i: (b, 2 * n_heads + h)),
            pl.BlockSpec((None, ATT_TQ, ATT_TK), lambda b, h, i: (scored(i) % tiles_per_chunk, 0, 0)),
            pl.BlockSpec((4, HEAD_DIM), fixed),
            pl.BlockSpec((1, LANES), fixed),
        ],
        out_specs=pl.BlockSpec((ATT_TQ, LANES), lambda b, h, i: (b * nq + finished(i), h)),
        out_shape=jax.ShapeDtypeStruct((batch * seq, n_heads * LANES), jnp.bfloat16),
        scratch_shapes=[
            pltpu.VMEM((seq, 2 * LANES), jnp.bfloat16),
            pltpu.VMEM((seq, 2 * LANES), jnp.bfloat16),
            pltpu.VMEM((4, ATT_TQ, 2 * LANES), jnp.bfloat16),
            pltpu.VMEM((n_chunks, ATT_TQ, ATT_TK), jnp.float32),
            pltpu.VMEM((n_chunks, ATT_TQ, ATT_TK), jnp.float32),
            pltpu.VMEM((2, ATT_TQ, LANES), jnp.float32),
            pltpu.VMEM((2, ATT_TQ, LANES), jnp.float32),
            pltpu.VMEM((2, ATT_TQ, 2 * LANES), jnp.float32),
        ],
        compiler_params=_cparams(("parallel", "parallel", "arbitrary")),
        name="diff_attn",
    )(slopes, qkv, qf, qkv, kf, qkv, diag, lam_vec, subln_g)


def _na_row_offset(cls, qr, kr):
    if cls == 0:
        return kr - qr + NA_KH - 1 if kr < NA_KH else None
    if cls == 2:
        return kr - qr - 1 if kr >= NA_KROWS - NA_KH else None
    return kr - qr + NA_KH // 2 - 1 if 0 <= kr - qr < NA_KH else None


def _na_build_bias(rpb_ref, bias_scr):
    shape = (GRID_W, LANES)
    qc = lax.broadcasted_iota(jnp.int32, shape, 0)
    lane = lax.broadcasted_iota(jnp.int32, shape, 1)
    kc = lane % GRID_W
    col_start = jnp.clip(qc - NA_KW // 2, 0, GRID_W - NA_KW)
    in_cols = (kc >= col_start) & (kc < col_start + NA_KW)
    left = lane < GRID_W
    neg = jnp.full(shape, NEG_BIG, jnp.float32)

    def toeplitz(hh, ro, lane_off):
        vec = jnp.broadcast_to(rpb_ref[hh, ro:ro + 1, :], shape)
        shift = (lane_off - (NA_KW - 1)) % LANES
        return pltpu.roll(vec, shift, 1, stride=1, stride_axis=0)

    for cls in range(3):
        for hh in range(2):
            for qr in range(NA_QROWS):
                for kp in range(NA_KROWS // 2):
                    ro_a = _na_row_offset(cls, qr, 2 * kp)
                    ro_b = _na_row_offset(cls, qr, 2 * kp + 1)
                    blk_a = neg if ro_a is None else jnp.where(in_cols, toeplitz(hh, ro_a, 0), neg)
                    blk_b = neg if ro_b is None else jnp.where(in_cols, toeplitz(hh, ro_b, GRID_W), neg)
                    bias_scr[cls, hh, qr * GRID_W:(qr + 1) * GRID_W, kp * LANES:(kp + 1) * LANES] = (
                        jnp.where(left, blk_a, blk_b))


def _na_kernel(q_ref, k_ref, v_ref, rpb_ref, o_ref, bias_scr, *, n_tiles):
    b = pl.program_id(1)
    t = pl.program_id(2)

    @pl.when((b == 0) & (t == 0))
    def _():
        _na_build_bias(rpb_ref, bias_scr)

    n_keys = NA_KROWS * GRID_W
    tq = NA_QROWS * GRID_W
    rows_total = n_tiles * NA_QROWS
    lane = lax.broadcasted_iota(jnp.int32, (tq, LANES), 1)
    ones = jnp.ones((n_keys, LANES), jnp.bfloat16)

    for sub in range(NA_TILES_PER_STEP):
        tile = t * NA_TILES_PER_STEP + sub
        ws = jnp.clip(tile * NA_QROWS - NA_KH // 2, 0, rows_total - NA_KROWS)
        start = pl.multiple_of(ws * GRID_W, GRID_W)
        kw = k_ref[pl.ds(start, n_keys), :]
        vw = jnp.concatenate([v_ref[pl.ds(start, n_keys), :], ones], axis=1)
        cls = jnp.where(tile == 0, 0, jnp.where(tile == n_tiles - 1, 2, 1))

        q = q_ref[sub * tq:(sub + 1) * tq, :]
        zero = jnp.zeros_like(q)
        outs = []
        for hh in range(2):
            qm = jnp.where((lane >= hh * HEAD_DIM) & (lane < (hh + 1) * HEAD_DIM), q, zero)
            s = lax.dot_general(qm, kw, (((1,), (1,)), ((), ())), preferred_element_type=jnp.float32)
            s = s + bias_scr[cls, hh]
            p = jnp.exp(s - jnp.max(s, axis=-1, keepdims=True))
            pv = jnp.dot(p.astype(jnp.bfloat16), vw, preferred_element_type=jnp.float32)
            outs.append(pv[:, :LANES] / pv[:, LANES:])
        o_ref[sub * tq:(sub + 1) * tq, :] = jnp.where(lane < HEAD_DIM, outs[0], outs[1]).astype(o_ref.dtype)


def _neighborhood_attention(qkv, rpb_pad, batch, seq):
    n_pairs = qkv.shape[1] // (3 * LANES)
    tq = NA_QROWS * GRID_W
    n_tiles = seq // tq
    n_steps = n_tiles // NA_TILES_PER_STEP
    blk = NA_TILES_PER_STEP * tq
    assert seq // GRID_W >= NA_KROWS and seq % blk == 0
    kern = functools.partial(_na_kernel, n_tiles=n_tiles)
    return pl.pallas_call(
        kern,
        grid=(n_pairs, batch, n_steps),
        in_specs=[
            pl.BlockSpec((blk, LANES), lambda hp, b, t: (b * n_steps + t, hp)),
            pl.BlockSpec((seq, LANES), lambda hp, b, t: (b, n_pairs + hp)),
            pl.BlockSpec((seq, LANES), lambda hp, b, t: (b, 2 * n_pairs + hp)),
            pl.BlockSpec((2, 2 * NA_KH - 1, LANES), lambda hp, b, t: (hp, 0, 0)),
        ],
        out_specs=pl.BlockSpec((blk, LANES), lambda hp, b, t: (b * n_steps + t, hp)),
        out_shape=jax.ShapeDtypeStruct((batch * seq, n_pairs * LANES), jnp.bfloat16),
        scratch_shapes=[pltpu.VMEM((3, 2, tq, NA_KROWS * GRID_W), jnp.float32)],
        compiler_params=_cparams(("arbitrary", "arbitrary", "arbitrary")),
        name="na_attn",
    )(qkv, qkv, qkv, rpb_pad)


def _ffn_kernel(x_ref, xp_ref, xn_ref, wu_ref, wg_ref, bu_ref, bg_ref, cwu_ref, cwg_ref,
                cbu_ref, cbg_ref, wo_ref, bo_ref, g_ref, beta_ref, y_ref, xb_scr, hu_scr, hg_scr,
                acc_scr, *, tiles_per_seq):
    i = pl.program_id(0)
    j = pl.program_id(1)
    tm = x_ref.shape[0]
    halo = FFN_HALO

    @pl.when(j == 0)
    def _():
        xb_scr[:halo, :] = xp_ref[...].astype(jnp.bfloat16)
        xb_scr[halo:halo + tm, :] = x_ref[...].astype(jnp.bfloat16)
        xb_scr[halo + tm:, :] = xn_ref[...].astype(jnp.bfloat16)
        acc_scr[...] = jnp.zeros_like(acc_scr)

    has_prev = (i % tiles_per_seq != 0).astype(jnp.float32)
    has_next = (i % tiles_per_seq != tiles_per_seq - 1).astype(jnp.float32)
    n_sub = tm // FFN_RC

    def up_proj(r, w_ref, b_ref, h_scr):
        lo = 0 if r == 0 else r * FFN_RC + 2 * halo
        hi = (r + 1) * FFN_RC + 2 * halo
        h = jnp.dot(xb_scr[lo:hi, :], w_ref[...], preferred_element_type=jnp.float32) + b_ref[...]
        if r == 0:
            h_scr[:halo, :] = h[:halo] * has_prev
            h_scr[halo:hi, :] = h[halo:]
        elif r == n_sub - 1:
            h_scr[lo:hi - halo, :] = h[:hi - halo - lo]
            h_scr[hi - halo:hi, :] = h[hi - halo - lo:] * has_next
        else:
            h_scr[lo:hi, :] = h

    def conv(r, cw_ref, cb_ref, h_scr):
        cw = cw_ref[...]
        base = r * FFN_RC + halo
        return (h_scr[pl.ds(base - 1, FFN_RC), :] * cw[0:1, :] + h_scr[pl.ds(base, FFN_RC), :] * cw[1:2, :]
                + h_scr[pl.ds(base + 1, FFN_RC), :] * cw[2:3, :] + cb_ref[...])

    up_proj(0, wu_ref, bu_ref, hu_scr)
    up_proj(0, wg_ref, bg_ref, hg_scr)
    for r in range(n_sub):
        if r + 1 < n_sub:
            up_proj(r + 1, wu_ref, bu_ref, hu_scr)
            up_proj(r + 1, wg_ref, bg_ref, hg_scr)
        u = conv(r, cwu_ref, cbu_ref, hu_scr)
        g = conv(r, cwg_ref, cbg_ref, hg_scr)
        gelu = 0.5 * g * (1.0 + lax.erf(g * (1.0 / math.sqrt(2.0))))
        a = (u * gelu).astype(jnp.bfloat16)
        acc_scr[r * FFN_RC:(r + 1) * FFN_RC, :] += jnp.dot(a, wo_ref[...],
                                                         preferred_element_type=jnp.float32)

    @pl.when(j == pl.num_programs(1) - 1)
    def _():
        z = ALPHA * x_ref[...] + acc_scr[...] + bo_ref[...]
        y_ref[...] = _layer_norm_rows(z, g_ref[...], beta_ref[...])


def _conv_ffn_ln(x, seq, w_in, b_in, conv_w, conv_b, w_out, b_out, g, beta):
    m, d = x.shape
    d_ff = w_out.shape[0]
    nj = d_ff // FFN_TN
    n_i = m // FFN_TM
    tiles_per_seq = seq // FFN_TM
    assert seq % FFN_TM == 0 and FFN_TM % FFN_RC == 0 and FFN_TM // FFN_RC >= 2
    halo_per_tile = FFN_TM // FFN_HALO
    n_halo_blocks = m // FFN_HALO
    u_col = lambda i, j: (0, j)
    g_col = lambda i, j: (0, nj + j)
    fixed = lambda i, j: (0, 0)
    kern = functools.partial(_ffn_kernel, tiles_per_seq=tiles_per_seq)
    h_shape = (FFN_TM + 2 * FFN_HALO, FFN_TN)
    return pl.pallas_call(
        kern,
        grid=(n_i, nj),
        in_specs=[
            pl.BlockSpec((FFN_TM, d), lambda i, j: (i, 0)),
            pl.BlockSpec((FFN_HALO, d), lambda i, j: (jnp.maximum(i * halo_per_tile - 1, 0), 0)),
            pl.BlockSpec((FFN_HALO, d),
                         lambda i, j: (jnp.minimum((i + 1) * halo_per_tile, n_halo_blocks - 1), 0)),
            pl.BlockSpec((d, FFN_TN), u_col),
            pl.BlockSpec((d, FFN_TN), g_col),
            pl.BlockSpec((1, FFN_TN), u_col),
            pl.BlockSpec((1, FFN_TN), g_col),
            pl.BlockSpec((3, FFN_TN), u_col),
            pl.BlockSpec((3, FFN_TN), g_col),
            pl.BlockSpec((1, FFN_TN), u_col),
            pl.BlockSpec((1, FFN_TN), g_col),
            pl.BlockSpec((FFN_TN, d), lambda i, j: (j, 0)),
            pl.BlockSpec((1, d), fixed),
            pl.BlockSpec((1, d), fixed),
            pl.BlockSpec((1, d), fixed),
        ],
        out_specs=pl.BlockSpec((FFN_TM, d), lambda i, j: (i, 0)),
        out_shape=jax.ShapeDtypeStruct((m, d), jnp.float32),
        scratch_shapes=[
            pltpu.VMEM((FFN_TM + 2 * FFN_HALO, d), jnp.bfloat16),
            pltpu.VMEM(h_shape, jnp.float32),
            pltpu.VMEM(h_shape, jnp.float32),
            pltpu.VMEM((FFN_TM, d), jnp.float32),
        ],
        compiler_params=_cparams(("parallel", "arbitrary")),
        name="conv_ffn_ln",
    )(x, x, x, w_in, w_in, b_in, b_in, conv_w, conv_w, conv_b, conv_b, w_out, b_out, g, beta)


def kernel(x, attn_w_qkv, attn_w_o, attn_lambda, attn_subln_g, na_w_qkv, na_b_qkv, na_rpb,
           na_w_o, na_b_o, ffn_w_in, ffn_b_in, ffn_conv_w, ffn_conv_b, ffn_w_out, ffn_b_out,
           ln_g, ln_b):
    batch, seq, d = x.shape
    bf16 = jnp.bfloat16
    f32 = jnp.float32
    xf = x.reshape(batch * seq, d)
    q_scale = HEAD_DIM ** -0.5

    n_attn_heads = attn_w_qkv.shape[2] // (3 * 2 * HEAD_DIM)
    slopes = jnp.exp2(-8.0 * jnp.arange(1, n_attn_heads + 1, dtype=f32) / n_attn_heads)
    qk_cols = 2 * n_attn_heads * HEAD_DIM
    attn_scale = jnp.where(jnp.arange(attn_w_qkv.shape[2]) < qk_cols, q_scale, 1.0).astype(f32)[None]
    na_dim = na_w_qkv.shape[2] // 3
    na_scale = jnp.where(jnp.arange(3 * na_dim) < na_dim, q_scale, 1.0).astype(f32)[None]
    zeros_d = jnp.zeros((1, d), f32)
    qf, kf = _alibi_features(n_attn_heads, seq)
    diag = _diag_correction()

    for i in range(DEPTH):
        j = i // 2
        if i % 2 == 0:
            lambda_init = 0.8 - 0.6 * math.exp(-0.3 * i)
            w = attn_w_qkv[j].astype(bf16)
            qkv = _qkv_proj(xf, w, jnp.zeros((1, w.shape[1]), f32), attn_scale)
            o = _diff_attention(qkv, slopes, qf, kf, diag, attn_lambda[j], attn_subln_g[j][None],
                                batch, seq, lambda_init)
            w_o, b_o = attn_w_o[j].astype(bf16), zeros_d
        else:
            qkv = _qkv_proj(xf, na_w_qkv[j].astype(bf16), na_b_qkv[j][None], na_scale)
            rpb_pad = jnp.pad(na_rpb[j], ((0, 0), (0, 0), (0, LANES - na_rpb.shape[3])))
            o = _neighborhood_attention(qkv, rpb_pad, batch, seq)
            w_o, b_o = na_w_o[j].astype(bf16), na_b_o[j][None]
        xf = _out_proj_ln(o, xf, w_o, b_o, ln_g[i, 0][None], ln_b[i, 0][None])
        xf = _conv_ffn_ln(xf, seq, ffn_w_in[i].astype(bf16), ffn_b_in[i][None], ffn_conv_w[i],
                          ffn_conv_b[i][None], ffn_w_out[i].astype(bf16), ffn_b_out[i][None],
                          ln_g[i, 1][None], ln_b[i, 1][None])
    return xf.reshape(batch, seq, d)
```

```python
import functools
import math

import jax
import jax.numpy as jnp
import numpy as np
from jax import lax
from jax.experimental import pallas as pl
from jax.experimental.pallas import tpu as pltpu

DEPTH = 4
HEAD_DIM = 64
GRID_W = 64
NA_KH = 8
NA_KW = 16
LN_EPS = 1e-5
RMS_EPS = 1e-5
ALPHA = (2.0 * DEPTH) ** 0.25

LANES = 128
VMEM_LIMIT_BYTES = 56 * 1024 * 1024

NEG_BIG = -1e30

PROJ_TM = 1024
PROJ_TN = 3072
OUT_TM = 512
FFN_TM = 1024
FFN_TN = 256
FFN_HALO = 16
FFN_RC = 512
ATT_TQ = 512
ATT_TK = 1024
ATT_UNROLL = 4
POS_SPLIT = 256
NA_QROWS = 4
NA_KROWS = NA_QROWS + NA_KH
NA_TILES_PER_STEP = 16


def _cparams(sem):
    return pltpu.CompilerParams(dimension_semantics=sem, vmem_limit_bytes=VMEM_LIMIT_BYTES)


def _layer_norm_rows(z, g, b):
    mu = jnp.mean(z, axis=-1, keepdims=True)
    zc = z - mu
    var = jnp.mean(zc * zc, axis=-1, keepdims=True)
    return zc * lax.rsqrt(var + LN_EPS) * g + b


def _proj_kernel(x_ref, w_ref, b_ref, s_ref, o_ref):
    acc = jnp.dot(x_ref[...].astype(jnp.bfloat16), w_ref[...], preferred_element_type=jnp.float32)
    o_ref[...] = ((acc + b_ref[...]) * s_ref[...]).astype(o_ref.dtype)


def _qkv_proj(x, w, bias, colscale):
    m, d = x.shape
    n = w.shape[1]
    return pl.pallas_call(
        _proj_kernel,
        grid=(m // PROJ_TM, n // PROJ_TN),
        in_specs=[
            pl.BlockSpec((PROJ_TM, d), lambda i, j: (i, 0)),
            pl.BlockSpec((d, PROJ_TN), lambda i, j: (0, j)),
            pl.BlockSpec((1, PROJ_TN), lambda i, j: (0, j)),
            pl.BlockSpec((1, PROJ_TN), lambda i, j: (0, j)),
        ],
        out_specs=pl.BlockSpec((PROJ_TM, PROJ_TN), lambda i, j: (i, j)),
        out_shape=jax.ShapeDtypeStruct((m, n), jnp.bfloat16),
        compiler_params=_cparams(("parallel", "arbitrary")),
        name="qkv_proj",
    )(x, w, bias, colscale)


def _out_ln_kernel(o_ref, x_ref, w_ref, b_ref, g_ref, beta_ref, y_ref):
    mix = jnp.dot(o_ref[...], w_ref[...], preferred_element_type=jnp.float32) + b_ref[...]
    z = ALPHA * x_ref[...] + mix
    y_ref[...] = _layer_norm_rows(z, g_ref[...], beta_ref[...])


def _out_proj_ln(o, x, w, b, g, beta):
    m, d = x.shape
    k = o.shape[1]
    row = lambda i: (i, 0)
    fixed = lambda i: (0, 0)
    return pl.pallas_call(
        _out_ln_kernel,
        grid=(m // OUT_TM,),
        in_specs=[
            pl.BlockSpec((OUT_TM, k), row),
            pl.BlockSpec((OUT_TM, d), row),
            pl.BlockSpec((k, d), fixed),
            pl.BlockSpec((1, d), fixed),
            pl.BlockSpec((1, d), fixed),
            pl.BlockSpec((1, d), fixed),
        ],
        out_specs=pl.BlockSpec((OUT_TM, d), row),
        out_shape=jax.ShapeDtypeStruct((m, d), jnp.float32),
        compiler_params=_cparams(("parallel",)),
        name="out_proj_ln",
    )(o, x, w, b, g, beta)


def _alibi_features(n_heads, seq):
    assert 8 % n_heads == 0 and seq <= POS_SPLIT * POS_SPLIT
    m = np.exp2(-8.0 * np.arange(1, n_heads + 1, dtype=np.float32) / n_heads).astype(np.float32)[:, None]
    pos = np.arange(seq)
    hi = ((pos // POS_SPLIT) * POS_SPLIT).astype(np.float32)[None]
    lo = (pos % POS_SPLIT).astype(np.float32)[None]
    qf = np.zeros((n_heads, seq, LANES), np.float32)
    kf = np.zeros((n_heads, seq, LANES), np.float32)
    qf[..., 0], qf[..., 1], qf[..., 2], qf[..., 3] = -m * hi, -m * lo, 1.0, 1.0
    kf[..., 0], kf[..., 1], kf[..., 2], kf[..., 3] = 1.0, 1.0, m * hi, m * lo
    return jnp.asarray(qf.astype(jnp.bfloat16)), jnp.asarray(kf.astype(jnp.bfloat16))


def _diag_correction():
    t = np.arange(ATT_TQ, dtype=np.float32)[None, :, None]
    s = np.arange(ATT_TK, dtype=np.float32)[None, None, :]
    off = (np.arange(ATT_TK // ATT_TQ, dtype=np.float32) * ATT_TQ)[:, None, None]
    return jnp.asarray(2.0 * np.maximum(s - t - off, 0.0))


def _diff_attn_kernel(slopes_ref, q_ref, qf_ref, k_ref, kf_ref, v_ref, diag_ref, lam_ref, g_ref,
                      o_ref, ka_scr, va_scr, qa_scr, s0_scr, s1_scr, mrun_scr, mfin_scr, acc_scr,
                      *, lambda_init, n_chunks, n_tiles):
    h = pl.program_id(1)
    i = pl.program_id(2)
    tq = q_ref.shape[0]
    diag_chunk = i // (ATT_TK // ATT_TQ)
    blocks = ATT_TK // LANES
    s_scrs = (s0_scr, s1_scr)

    @pl.when(i == 0)
    def _():
        ka_scr[:, :LANES] = k_ref[...]
        ka_scr[:, LANES:] = kf_ref[...]
        va_scr[:, :LANES] = v_ref[...]
        va_scr[:, LANES:] = jnp.ones(v_ref.shape, v_ref.dtype)
        acc_scr[...] = jnp.zeros_like(acc_scr)

    @pl.when(i < n_tiles)
    def _():
        q = q_ref[...]
        qf = qf_ref[...]
        lane = lax.broadcasted_iota(jnp.int32, q.shape, 1)
        zero = jnp.zeros_like(q)
        for c in range(2):
            qc = jnp.where((lane >= c * HEAD_DIM) & (lane < (c + 1) * HEAD_DIM), q, zero)
            qa_scr[2 * c] = jnp.concatenate([qc, qf], axis=1)
            qa_scr[2 * c + 1] = jnp.concatenate([qc, -qf], axis=1)
        mrun_scr[...] = jnp.full(mrun_scr.shape, -jnp.inf, jnp.float32)

    def score_one(c, j):
        start = pl.multiple_of(j * ATT_TK, ATT_TK)
        after = (j > diag_chunk).astype(jnp.int32)
        s = lax.dot_general(qa_scr[2 * c + after], ka_scr[pl.ds(start, ATT_TK), :],
                            (((1,), (1,)), ((), ())), preferred_element_type=jnp.float32)
        coef = jnp.where(j == diag_chunk, slopes_ref[h], 0.0)
        s = s - coef * diag_ref[...]
        s_scrs[c][j] = s
        mx = s[:, :LANES]
        for kb in range(1, blocks):
            mx = jnp.maximum(mx, s[:, kb * LANES:(kb + 1) * LANES])
        mrun_scr[c] = jnp.maximum(mrun_scr[c], mx)

    def pv_one(c, j):
        start = pl.multiple_of(j * ATT_TK, ATT_TK)
        s = s_scrs[c][j]
        mb = mfin_scr[c]
        p = jnp.concatenate(
            [jnp.exp(s[:, kb * LANES:(kb + 1) * LANES] - mb).astype(jnp.bfloat16) for kb in range(blocks)],
            axis=1)
        acc_scr[c] += jnp.dot(p, va_scr[pl.ds(start, ATT_TK), :], preferred_element_type=jnp.float32)

    def run_chunks(score_c, pv_c):
        def body(jj, carry):
            for u in range(ATT_UNROLL):
                j = jj * ATT_UNROLL + u
                if score_c is not None:
                    score_one(score_c, j)
                if pv_c is not None:
                    pv_one(pv_c, j)
            return carry
        lax.fori_loop(0, n_chunks // ATT_UNROLL, body, 0)

    def finish_max(c):
        mfin_scr[c] = jnp.broadcast_to(jnp.max(mrun_scr[c], axis=-1, keepdims=True), (tq, LANES))

    @pl.when(i == 0)
    def _():
        run_chunks(0, None)

    @pl.when((i > 0) & (i < n_tiles))
    def _():
        run_chunks(0, 1)

    @pl.when(i == n_tiles)
    def _():
        run_chunks(None, 1)

    @pl.when(i > 0)
    def _():
        lf = lam_ref[...]
        lam = (jnp.exp(jnp.sum(lf[0:1, :] * lf[1:2, :], axis=-1, keepdims=True))
               - jnp.exp(jnp.sum(lf[2:3, :] * lf[3:4, :], axis=-1, keepdims=True)) + lambda_init)
        a1 = acc_scr[0]
        a2 = acc_scr[1]
        o = a1[:, :LANES] / a1[:, LANES:] - lam * (a2[:, :LANES] / a2[:, LANES:])
        y = o * lax.rsqrt(jnp.mean(o * o, axis=-1, keepdims=True) + RMS_EPS) * g_ref[...]
        o_ref[...] = (y * (1.0 - lambda_init)).astype(o_ref.dtype)
        acc_scr[...] = jnp.zeros_like(acc_scr)

    @pl.when(i < n_tiles)
    def _():
        finish_max(0)
        run_chunks(1, 0)
        finish_max(1)


def _diff_attention(qkv, slopes, qf, kf, diag, lam_vec, subln_g, batch, seq, lambda_init):
    n_heads = qkv.shape[1] // (3 * LANES)
    assert ATT_TK % ATT_TQ == 0 and seq % (ATT_TK * ATT_UNROLL) == 0
    nq = seq // ATT_TQ
    n_chunks = seq // ATT_TK
    tiles_per_chunk = ATT_TK // ATT_TQ
    kern = functools.partial(_diff_attn_kernel, lambda_init=lambda_init, n_chunks=n_chunks, n_tiles=nq)
    fixed = lambda b, h, i: (0, 0)
    scored = lambda i: jnp.minimum(i, nq - 1)
    finished = lambda i: jnp.maximum(i - 1, 0)
    return pl.pallas_call(
        kern,
        grid=(batch, n_heads, nq + 1),
        in_specs=[
            pl.BlockSpec(memory_space=pltpu.SMEM),
            pl.BlockSpec((ATT_TQ, LANES), lambda b, h, i: (b * nq + scored(i), h)),
            pl.BlockSpec((None, ATT_TQ, LANES), lambda b, h, i: (h, scored(i), 0)),
            pl.BlockSpec((seq, LANES), lambda b, h, i: (b, n_heads + h)),
            pl.BlockSpec((None, seq, LANES), lambda b, h, i: (h, 0, 0)),
            pl.BlockSpec((seq, LANES), lambda b, h, i: (b, 2 * n_heads + h)),
            pl.BlockSpec((None, ATT_TQ, ATT_TK), lambda b, h, i: (scored(i) % tiles_per_chunk, 0, 0)),
            pl.BlockSpec((4, HEAD_DIM), fixed),
            pl.BlockSpec((1, LANES), fixed),
        ],
        out_specs=pl.BlockSpec((ATT_TQ, LANES), lambda b, h, i: (b * nq + finished(i), h)),
        out_shape=jax.ShapeDtypeStruct((batch * seq, n_heads * LANES), jnp.bfloat16),
        scratch_shapes=[
            pltpu.VMEM((seq, 2 * LANES), jnp.bfloat16),
            pltpu.VMEM((seq, 2 * LANES), jnp.bfloat16),
            pltpu.VMEM((4, ATT_TQ, 2 * LANES), jnp.bfloat16),
            pltpu.VMEM((n_chunks, ATT_TQ, ATT_TK), jnp.float32),
            pltpu.VMEM((n_chunks, ATT_TQ, ATT_TK), jnp.float32),
            pltpu.VMEM((2, ATT_TQ, LANES), jnp.float32),
            pltpu.VMEM((2, ATT_TQ, LANES), jnp.float32),
            pltpu.VMEM((2, ATT_TQ, 2 * LANES), jnp.float32),
        ],
        compiler_params=_cparams(("parallel", "parallel", "arbitrary")),
        name="diff_attn",
    )(slopes, qkv, qf, qkv, kf, qkv, diag, lam_vec, subln_g)


def _na_row_offset(cls, qr, kr):
    if cls == 0:
        return kr - qr + NA_KH - 1 if kr < NA_KH else None
    if cls == 2:
        return kr - qr - 1 if kr >= NA_KROWS - NA_KH else None
    return kr - qr + NA_KH // 2 - 1 if 0 <= kr - qr < NA_KH else None


def _na_build_bias(rpb_ref, bias_scr):
    shape = (GRID_W, LANES)
    qc = lax.broadcasted_iota(jnp.int32, shape, 0)
    lane = lax.broadcasted_iota(jnp.int32, shape, 1)
    kc = lane % GRID_W
    col_start = jnp.clip(qc - NA_KW // 2, 0, GRID_W - NA_KW)
    in_cols = (kc >= col_start) & (kc < col_start + NA_KW)
    left = lane < GRID_W
    neg = jnp.full(shape, NEG_BIG, jnp.float32)

    def toeplitz(hh, ro, lane_off):
        vec = jnp.broadcast_to(rpb_ref[hh, ro:ro + 1, :], shape)
        shift = (lane_off - (NA_KW - 1)) % LANES
        return pltpu.roll(vec, shift, 1, stride=1, stride_axis=0)

    for cls in range(3):
        for hh in range(2):
            for qr in range(NA_QROWS):
                for kp in range(NA_KROWS // 2):
                    ro_a = _na_row_offset(cls, qr, 2 * kp)
                    ro_b = _na_row_offset(cls, qr, 2 * kp + 1)
                    blk_a = neg if ro_a is None else jnp.where(in_cols, toeplitz(hh, ro_a, 0), neg)
                    blk_b = neg if ro_b is None else jnp.where(in_cols, toeplitz(hh, ro_b, GRID_W), neg)
                    bias_scr[cls, hh, qr * GRID_W:(qr + 1) * GRID_W, kp * LANES:(kp + 1) * LANES] = (
                        jnp.where(left, blk_a, blk_b))


def _na_kernel(q_ref, k_ref, v_ref, rpb_ref, o_ref, bias_scr, *, n_tiles):
    b = pl.program_id(1)
    t = pl.program_id(2)

    @pl.when((b == 0) & (t == 0))
    def _():
        _na_build_bias(rpb_ref, bias_scr)

    n_keys = NA_KROWS * GRID_W
    tq = NA_QROWS * GRID_W
    rows_total = n_tiles * NA_QROWS
    lane = lax.broadcasted_iota(jnp.int32, (tq, LANES), 1)
    ones = jnp.ones((n_keys, LANES), jnp.bfloat16)

    for sub in range(NA_TILES_PER_STEP):
        tile = t * NA_TILES_PER_STEP + sub
        ws = jnp.clip(tile * NA_QROWS - NA_KH // 2, 0, rows_total - NA_KROWS)
        start = pl.multiple_of(ws * GRID_W, GRID_W)
        kw = k_ref[pl.ds(start, n_keys), :]
        vw = jnp.concatenate([v_ref[pl.ds(start, n_keys), :], ones], axis=1)
        cls = jnp.where(tile == 0, 0, jnp.where(tile == n_tiles - 1, 2, 1))

        q = q_ref[sub * tq:(sub + 1) * tq, :]
        zero = jnp.zeros_like(q)
        outs = []
        for hh in range(2):
            qm = jnp.where((lane >= hh * HEAD_DIM) & (lane < (hh + 1) * HEAD_DIM), q, zero)
            s = lax.dot_general(qm, kw, (((1,), (1,)), ((), ())), preferred_element_type=jnp.float32)
            s = s + bias_scr[cls, hh]
            p = jnp.exp(s - jnp.max(s, axis=-1, keepdims=True))
            pv = jnp.dot(p.astype(jnp.bfloat16), vw, preferred_element_type=jnp.float32)
            outs.append(pv[:, :LANES] / pv[:, LANES:])
        o_ref[sub * tq:(sub + 1) * tq, :] = jnp.where(lane < HEAD_DIM, outs[0], outs[1]).astype(o_ref.dtype)


def _neighborhood_attention(qkv, rpb_pad, batch, seq):
    n_pairs = qkv.shape[1] // (3 * LANES)
    tq = NA_QROWS * GRID_W
    n_tiles = seq // tq
    n_steps = n_tiles // NA_TILES_PER_STEP
    blk = NA_TILES_PER_STEP * tq
    assert seq // GRID_W >= NA_KROWS and seq % blk == 0
    kern = functools.partial(_na_kernel, n_tiles=n_tiles)
    return pl.pallas_call(
        kern,
        grid=(n_pairs, batch, n_steps),
        in_specs=[
            pl.BlockSpec((blk, LANES), lambda hp, b, t: (b * n_steps + t, hp)),
            pl.BlockSpec((seq, LANES), lambda hp, b, t: (b, n_pairs + hp)),
            pl.BlockSpec((seq, LANES), lambda hp, b, t: (b, 2 * n_pairs + hp)),
            pl.BlockSpec((2, 2 * NA_KH - 1, LANES), lambda hp, b, t: (hp, 0, 0)),
        ],
        out_specs=pl.BlockSpec((blk, LANES), lambda hp, b, t: (b * n_steps + t, hp)),
        out_shape=jax.ShapeDtypeStruct((batch * seq, n_pairs * LANES), jnp.bfloat16),
        scratch_shapes=[pltpu.VMEM((3, 2, tq, NA_KROWS * GRID_W), jnp.float32)],
        compiler_params=_cparams(("arbitrary", "arbitrary", "arbitrary")),
        name="na_attn",
    )(qkv, qkv, qkv, rpb_pad)


def _ffn_kernel(x_ref, xp_ref, xn_ref, wu_ref, wg_ref, bu_ref, bg_ref, cwu_ref, cwg_ref,
                cbu_ref, cbg_ref, wo_ref, bo_ref, g_ref, beta_ref, y_ref, xb_scr, hu_scr, hg_scr,
                acc_scr, *, tiles_per_seq):
    i = pl.program_id(0)
    j = pl.program_id(1)
    tm = x_ref.shape[0]
    halo = FFN_HALO

    @pl.when(j == 0)
    def _():
        xb_scr[:halo, :] = xp_ref[...].astype(jnp.bfloat16)
        xb_scr[halo:halo + tm, :] = x_ref[...].astype(jnp.bfloat16)
        xb_scr[halo + tm:, :] = xn_ref[...].astype(jnp.bfloat16)
        acc_scr[...] = jnp.zeros_like(acc_scr)

    has_prev = (i % tiles_per_seq != 0).astype(jnp.float32)
    has_next = (i % tiles_per_seq != tiles_per_seq - 1).astype(jnp.float32)
    n_sub = tm // FFN_RC

    def up_proj(r, w_ref, b_ref, h_scr):
        lo = 0 if r == 0 else r * FFN_RC + 2 * halo
        hi = (r + 1) * FFN_RC + 2 * halo
        h = jnp.dot(xb_scr[lo:hi, :], w_ref[...], preferred_element_type=jnp.float32) + b_ref[...]
        if r == 0:
            h_scr[:halo, :] = h[:halo] * has_prev
            h_scr[halo:hi, :] = h[halo:]
        elif r == n_sub - 1:
            h_scr[lo:hi - halo, :] = h[:hi - halo - lo]
            h_scr[hi - halo:hi, :] = h[hi - halo - lo:] * has_next
        else:
            h_scr[lo:hi, :] = h

    def conv(r, cw_ref, cb_ref, h_scr):
        cw = cw_ref[...]
        base = r * FFN_RC + halo
        return (h_scr[pl.ds(base - 1, FFN_RC), :] * cw[0:1, :] + h_scr[pl.ds(base, FFN_RC), :] * cw[1:2, :]
                + h_scr[pl.ds(base + 1, FFN_RC), :] * cw[2:3, :] + cb_ref[...])

    up_proj(0, wu_ref, bu_ref, hu_scr)
    up_proj(0, wg_ref, bg_ref, hg_scr)
    for r in range(n_sub):
        if r + 1 < n_sub:
            up_proj(r + 1, wu_ref, bu_ref, hu_scr)
            up_proj(r + 1, wg_ref, bg_ref, hg_scr)
        u = conv(r, cwu_ref, cbu_ref, hu_scr)
        g = conv(r, cwg_ref, cbg_ref, hg_scr)
        gelu = 0.5 * g * (1.0 + lax.erf(g * (1.0 / math.sqrt(2.0))))
        a = (u * gelu).astype(jnp.bfloat16)
        acc_scr[r * FFN_RC:(r + 1) * FFN_RC, :] += jnp.dot(a, wo_ref[...],
                                                         preferred_element_type=jnp.float32)

    @pl.when(j == pl.num_programs(1) - 1)
    def _():
        z = ALPHA * x_ref[...] + acc_scr[...] + bo_ref[...]
        y_ref[...] = _layer_norm_rows(z, g_ref[...], beta_ref[...])


def _conv_ffn_ln(x, seq, w_in, b_in, conv_w, conv_b, w_out, b_out, g, beta):
    m, d = x.shape
    d_ff = w_out.shape[0]
    nj = d_ff // FFN_TN
    n_i = m // FFN_TM
    tiles_per_seq = seq // FFN_TM
    assert seq % FFN_TM == 0 and FFN_TM % FFN_RC == 0 and FFN_TM // FFN_RC >= 2
    halo_per_tile = FFN_TM // FFN_HALO
    n_halo_blocks = m // FFN_HALO
    u_col = lambda i, j: (0, j)
    g_col = lambda i, j: (0, nj + j)
    fixed = lambda i, j: (0, 0)
    kern = functools.partial(_ffn_kernel, tiles_per_seq=tiles_per_seq)
    h_shape = (FFN_TM + 2 * FFN_HALO, FFN_TN)
    return pl.pallas_call(
        kern,
        grid=(n_i, nj),
        in_specs=[
            pl.BlockSpec((FFN_TM, d), lambda i, j: (i, 0)),
            pl.BlockSpec((FFN_HALO, d), lambda i, j: (jnp.maximum(i * halo_per_tile - 1, 0), 0)),
            pl.BlockSpec((FFN_HALO, d),
                         lambda i, j: (jnp.minimum((i + 1) * halo_per_tile, n_halo_blocks - 1), 0)),
            pl.BlockSpec((d, FFN_TN), u_col),
            pl.BlockSpec((d, FFN_TN), g_col),
            pl.BlockSpec((1, FFN_TN), u_col),
            pl.BlockSpec((1, FFN_TN), g_col),
            pl.BlockSpec((3, FFN_TN), u_col),
            pl.BlockSpec((3, FFN_TN), g_col),
            pl.BlockSpec((1, FFN_TN), u_col),
            pl.BlockSpec((1, FFN_TN), g_col),
            pl.BlockSpec((FFN_TN, d), lambda i, j: (j, 0)),
            pl.BlockSpec((1, d), fixed),
            pl.BlockSpec((1, d), fixed),
            pl.BlockSpec((1, d), fixed),
        ],
        out_specs=pl.BlockSpec((FFN_TM, d), lambda i, j: (i, 0)),
        out_shape=jax.ShapeDtypeStruct((m, d), jnp.float32),
        scratch_shapes=[
            pltpu.VMEM((FFN_TM + 2 * FFN_HALO, d), jnp.bfloat16),
            pltpu.VMEM(h_shape, jnp.float32),
            pltpu.VMEM(h_shape, jnp.float32),
            pltpu.VMEM((FFN_TM, d), jnp.float32),
        ],
        compiler_params=_cparams(("parallel", "arbitrary")),
        name="conv_ffn_ln",
    )(x, x, x, w_in, w_in, b_in, b_in, conv_w, conv_w, conv_b, conv_b, w_out, b_out, g, beta)


def kernel(x, attn_w_qkv, attn_w_o, attn_lambda, attn_subln_g, na_w_qkv, na_b_qkv, na_rpb,
           na_w_o, na_b_o, ffn_w_in, ffn_b_in, ffn_conv_w, ffn_conv_b, ffn_w_out, ffn_b_out,
           ln_g, ln_b):
    batch, seq, d = x.shape
    bf16 = jnp.bfloat16
    f32 = jnp.float32
    xf = x.reshape(batch * seq, d)
    q_scale = HEAD_DIM ** -0.5

    n_attn_heads = attn_w_qkv.shape[2] // (3 * 2 * HEAD_DIM)
    slopes = jnp.exp2(-8.0 * jnp.arange(1, n_attn_heads + 1, dtype=f32) / n_attn_heads)
    qk_cols = 2 * n_attn_heads * HEAD_DIM
    attn_scale = jnp.where(jnp.arange(attn_w_qkv.shape[2]) < qk_cols, q_scale, 1.0).astype(f32)[None]
    na_dim = na_w_qkv.shape[2] // 3
    na_scale = jnp.where(jnp.arange(3 * na_dim) < na_dim, q_scale, 1.0).astype(f32)[None]
    zeros_d = jnp.zeros((1, d), f32)
    qf, kf = _alibi_features(n_attn_heads, seq)
    diag = _diag_correction()

    for i in range(DEPTH):
        j = i // 2
        if i % 2 == 0:
            lambda_init = 0.8 - 0.6 * math.exp(-0.3 * i)
            w = attn_w_qkv[j].astype(bf16)
            qkv = _qkv_proj(xf, w, jnp.zeros((1, w.shape[1]), f32), attn_scale)
            o = _diff_attention(qkv, slopes, qf, kf, diag, attn_lambda[j], attn_subln_g[j][None],
                                batch, seq, lambda_init)
            w_o, b_o = attn_w_o[j].astype(bf16), zeros_d
        else:
            qkv = _qkv_proj(xf, na_w_qkv[j].astype(bf16), na_b_qkv[j][None], na_scale)
            rpb_pad = jnp.pad(na_rpb[j], ((0, 0), (0, 0), (0, LANES - na_rpb.shape[3])))
            o = _neighborhood_attention(qkv, rpb_pad, batch, seq)
            w_o, b_o = na_w_o[j].astype(bf16), na_b_o[j][None]
        xf = _out_proj_ln(o, xf, w_o, b_o, ln_g[i, 0][None], ln_b[i, 0][None])
        xf = _conv_ffn_ln(xf, seq, ffn_w_in[i].astype(bf16), ffn_b_in[i][None], ffn_conv_w[i],
                          ffn_conv_b[i][None], ffn_w_out[i].astype(bf16), ffn_b_out[i][None],
                          ln_g[i, 1][None], ln_b[i, 1][None])
    return xf.reshape(batch, seq, d)
```

```python
import functools
import math

import jax
import jax.numpy as jnp
import numpy as np
from jax import lax
from jax.experimental import pallas as pl
from jax.experimental.pallas import tpu as pltpu

DEPTH = 4
HEAD_DIM = 64
GRID_W = 64
NA_KH = 8
NA_KW = 16
LN_EPS = 1e-5
RMS_EPS = 1e-5
ALPHA = (2.0 * DEPTH) ** 0.25

LANES = 128
VMEM_LIMIT_BYTES = 56 * 1024 * 1024

NEG_BIG = -1e30

PROJ_TM = 1024
PROJ_TN = 3072
OUT_TM = 1024
FFN_TM = 1024
FFN_TN = 256
FFN_HALO = 16
FFN_RC = 512
ATT_TQ = 512
ATT_TK = 1024
ATT_UNROLL = 4
POS_SPLIT = 256
NA_QROWS = 4
NA_KROWS = NA_QROWS + NA_KH
NA_TILES_PER_STEP = 16


def _cparams(sem):
    return pltpu.CompilerParams(dimension_semantics=sem, vmem_limit_bytes=VMEM_LIMIT_BYTES)


def _layer_norm_rows(z, g, b):
    mu = jnp.mean(z, axis=-1, keepdims=True)
    zc = z - mu
    var = jnp.mean(zc * zc, axis=-1, keepdims=True)
    return zc * lax.rsqrt(var + LN_EPS) * g + b


def _proj_kernel(x_ref, w_ref, b_ref, s_ref, o_ref):
    acc = jnp.dot(x_ref[...].astype(jnp.bfloat16), w_ref[...], preferred_element_type=jnp.float32)
    o_ref[...] = ((acc + b_ref[...]) * s_ref[...]).astype(o_ref.dtype)


def _qkv_proj(x, w, bias, colscale):
    m, d = x.shape
    n = w.shape[1]
    return pl.pallas_call(
        _proj_kernel,
        grid=(m // PROJ_TM, n // PROJ_TN),
        in_specs=[
            pl.BlockSpec((PROJ_TM, d), lambda i, j: (i, 0)),
            pl.BlockSpec((d, PROJ_TN), lambda i, j: (0, j)),
            pl.BlockSpec((1, PROJ_TN), lambda i, j: (0, j)),
            pl.BlockSpec((1, PROJ_TN), lambda i, j: (0, j)),
        ],
        out_specs=pl.BlockSpec((PROJ_TM, PROJ_TN), lambda i, j: (i, j)),
        out_shape=jax.ShapeDtypeStruct((m, n), jnp.bfloat16),
        compiler_params=_cparams(("parallel", "arbitrary")),
        name="qkv_proj",
    )(x, w, bias, colscale)


def _out_ln_kernel(o_ref, x_ref, w_ref, b_ref, g_ref, beta_ref, y_ref):
    mix = jnp.dot(o_ref[...], w_ref[...], preferred_element_type=jnp.float32) + b_ref[...]
    z = ALPHA * x_ref[...] + mix
    y_ref[...] = _layer_norm_rows(z, g_ref[...], beta_ref[...])


def _out_proj_ln(o, x, w, b, g, beta):
    m, d = x.shape
    k = o.shape[1]
    row = lambda i: (i, 0)
    fixed = lambda i: (0, 0)
    return pl.pallas_call(
        _out_ln_kernel,
        grid=(m // OUT_TM,),
        in_specs=[
            pl.BlockSpec((OUT_TM, k), row),
            pl.BlockSpec((OUT_TM, d), row),
            pl.BlockSpec((k, d), fixed),
            pl.BlockSpec((1, d), fixed),
            pl.BlockSpec((1, d), fixed),
            pl.BlockSpec((1, d), fixed),
        ],
        out_specs=pl.BlockSpec((OUT_TM, d), row),
        out_shape=jax.ShapeDtypeStruct((m, d), jnp.float32),
        compiler_params=_cparams(("parallel",)),
        name="out_proj_ln",
    )(o, x, w, b, g, beta)


def _alibi_features(n_heads, seq):
    assert 8 % n_heads == 0 and seq <= POS_SPLIT * POS_SPLIT
    m = np.exp2(-8.0 * np.arange(1, n_heads + 1, dtype=np.float32) / n_heads).astype(np.float32)[:, None]
    pos = np.arange(seq)
    hi = ((pos // POS_SPLIT) * POS_SPLIT).astype(np.float32)[None]
    lo = (pos % POS_SPLIT).astype(np.float32)[None]
    qf = np.zeros((n_heads, seq, LANES), np.float32)
    kf = np.zeros((n_heads, seq, LANES), np.float32)
    qf[..., 0], qf[..., 1], qf[..., 2], qf[..., 3] = -m * hi, -m * lo, 1.0, 1.0
    kf[..., 0], kf[..., 1], kf[..., 2], kf[..., 3] = 1.0, 1.0, m * hi, m * lo
    return jnp.asarray(qf.astype(jnp.bfloat16)), jnp.asarray(kf.astype(jnp.bfloat16))


def _diag_correction():
    t = np.arange(ATT_TQ, dtype=np.float32)[None, :, None]
    s = np.arange(ATT_TK, dtype=np.float32)[None, None, :]
    off = (np.arange(ATT_TK // ATT_TQ, dtype=np.float32) * ATT_TQ)[:, None, None]
    return jnp.asarray(2.0 * np.maximum(s - t - off, 0.0))


def _diff_attn_kernel(slopes_ref, q_ref, qf_ref, k_ref, kf_ref, v_ref, diag_ref, lam_ref, g_ref,
                      o_ref, ka_scr, va_scr, qa_scr, s0_scr, s1_scr, mrun_scr, mfin_scr, acc_scr,
                      *, lambda_init, n_chunks, n_tiles):
    h = pl.program_id(1)
    i = pl.program_id(2)
    tq = q_ref.shape[0]
    diag_chunk = i // (ATT_TK // ATT_TQ)
    blocks = ATT_TK // LANES
    s_scrs = (s0_scr, s1_scr)

    @pl.when(i == 0)
    def _():
        ka_scr[:, :LANES] = k_ref[...]
        ka_scr[:, LANES:] = kf_ref[...]
        va_scr[:, :LANES] = v_ref[...]
        va_scr[:, LANES:] = jnp.ones(v_ref.shape, v_ref.dtype)
        acc_scr[...] = jnp.zeros_like(acc_scr)

    @pl.when(i < n_tiles)
    def _():
        q = q_ref[...]
        qf = qf_ref[...]
        lane = lax.broadcasted_iota(jnp.int32, q.shape, 1)
        zero = jnp.zeros_like(q)
        for c in range(2):
            qc = jnp.where((lane >= c * HEAD_DIM) & (lane < (c + 1) * HEAD_DIM), q, zero)
            qa_scr[2 * c] = jnp.concatenate([qc, qf], axis=1)
            qa_scr[2 * c + 1] = jnp.concatenate([qc, -qf], axis=1)
        mrun_scr[...] = jnp.full(mrun_scr.shape, -jnp.inf, jnp.float32)

    def score_one(c, j):
        start = pl.multiple_of(j * ATT_TK, ATT_TK)
        after = (j > diag_chunk).astype(jnp.int32)
        s = lax.dot_general(qa_scr[2 * c + after], ka_scr[pl.ds(start, ATT_TK), :],
                            (((1,), (1,)), ((), ())), preferred_element_type=jnp.float32)
        coef = jnp.where(j == diag_chunk, slopes_ref[h], 0.0)
        s = s - coef * diag_ref[...]
        s_scrs[c][j] = s
        mx = s[:, :LANES]
        for kb in range(1, blocks):
            mx = jnp.maximum(mx, s[:, kb * LANES:(kb + 1) * LANES])
        mrun_scr[c] = jnp.maximum(mrun_scr[c], mx)

    def pv_one(c, j):
        start = pl.multiple_of(j * ATT_TK, ATT_TK)
        s = s_scrs[c][j]
        mb = mfin_scr[c]
        p = jnp.concatenate(
            [jnp.exp(s[:, kb * LANES:(kb + 1) * LANES] - mb).astype(jnp.bfloat16) for kb in range(blocks)],
            axis=1)
        acc_scr[c] += jnp.dot(p, va_scr[pl.ds(start, ATT_TK), :], preferred_element_type=jnp.float32)

    def run_chunks(score_c, pv_c):
        def body(jj, carry):
            for u in range(ATT_UNROLL):
                j = jj * ATT_UNROLL + u
                if score_c is not None:
                    score_one(score_c, j)
                if pv_c is not None:
                    pv_one(pv_c, j)
            return carry
        lax.fori_loop(0, n_chunks // ATT_UNROLL, body, 0)

    def finish_max(c):
        mfin_scr[c] = jnp.broadcast_to(jnp.max(mrun_scr[c], axis=-1, keepdims=True), (tq, LANES))

    @pl.when(i == 0)
    def _():
        run_chunks(0, None)

    @pl.when((i > 0) & (i < n_tiles))
    def _():
        run_chunks(0, 1)

    @pl.when(i == n_tiles)
    def _():
        run_chunks(None, 1)

    @pl.when(i > 0)
    def _():
        lf = lam_ref[...]
        lam = (jnp.exp(jnp.sum(lf[0:1, :] * lf[1:2, :], axis=-1, keepdims=True))
               - jnp.exp(jnp.sum(lf[2:3, :] * lf[3:4, :], axis=-1, keepdims=True)) + lambda_init)
        a1 = acc_scr[0]
        a2 = acc_scr[1]
        o = a1[:, :LANES] / a1[:, LANES:] - lam * (a2[:, :LANES] / a2[:, LANES:])
        y = o * lax.rsqrt(jnp.mean(o * o, axis=-1, keepdims=True) + RMS_EPS) * g_ref[...]
        o_ref[...] = (y * (1.0 - lambda_init)).astype(o_ref.dtype)
        acc_scr[...] = jnp.zeros_like(acc_scr)

    @pl.when(i < n_tiles)
    def _():
        finish_max(0)
        run_chunks(1, 0)
        finish_max(1)


def _diff_attention(qkv, slopes, qf, kf, diag, lam_vec, subln_g, batch, seq, lambda_init):
    n_heads = qkv.shape[1] // (3 * LANES)
    assert ATT_TK % ATT_TQ == 0 and seq % (ATT_TK * ATT_UNROLL) == 0
    nq = seq // ATT_TQ
    n_chunks = seq // ATT_TK
    tiles_per_chunk = ATT_TK // ATT_TQ
    kern = functools.partial(_diff_attn_kernel, lambda_init=lambda_init, n_chunks=n_chunks, n_tiles=nq)
    fixed = lambda b, h, i: (0, 0)
    scored = lambda i: jnp.minimum(i, nq - 1)
    finished = lambda i: jnp.maximum(i - 1, 0)
    return pl.pallas_call(
        kern,
        grid=(batch, n_heads, nq + 1),
        in_specs=[
            pl.BlockSpec(memory_space=pltpu.SMEM),
            pl.BlockSpec((ATT_TQ, LANES), lambda b, h, i: (b * nq + scored(i), h)),
            pl.BlockSpec((None, ATT_TQ, LANES), lambda b, h, i: (h, scored(i), 0)),
            pl.BlockSpec((seq, LANES), lambda b, h, i: (b, n_heads + h)),
            pl.BlockSpec((None, seq, LANES), lambda b, h, i: (h, 0, 0)),
            pl.BlockSpec((seq, LANES), lambda b, h, i: (b, 2 * n_heads + h)),
            pl.BlockSpec((None, ATT_TQ, ATT_TK), lambda b, h, i: (scored(i) % tiles_per_chunk, 0, 0)),
            pl.BlockSpec((4, HEAD_DIM), fixed),
            pl.BlockSpec((1, LANES), fixed),
        ],
        out_specs=pl.BlockSpec((ATT_TQ, LANES), lambda b, h, i: (b * nq + finished(i), h)),
        out_shape=jax.ShapeDtypeStruct((batch * seq, n_heads * LANES), jnp.bfloat16),
        scratch_shapes=[
            pltpu.VMEM((seq, 2 * LANES), jnp.bfloat16),
            pltpu.VMEM((seq, 2 * LANES), jnp.bfloat16),
            pltpu.VMEM((4, ATT_TQ, 2 * LANES), jnp.bfloat16),
            pltpu.VMEM((n_chunks, ATT_TQ, ATT_TK), jnp.float32),
            pltpu.VMEM((n_chunks, ATT_TQ, ATT_TK), jnp.float32),
            pltpu.VMEM((2, ATT_TQ, LANES), jnp.float32),
            pltpu.VMEM((2, ATT_TQ, LANES), jnp.float32),
            pltpu.VMEM((2, ATT_TQ, 2 * LANES), jnp.float32),
        ],
        compiler_params=_cparams(("parallel", "parallel", "arbitrary")),
        name="diff_attn",
    )(slopes, qkv, qf, qkv, kf, qkv, diag, lam_vec, subln_g)


def _na_row_offset(cls, qr, kr):
    if cls == 0:
        return kr - qr + NA_KH - 1 if kr < NA_KH else None
    if cls == 2:
        return kr - qr - 1 if kr >= NA_KROWS - NA_KH else None
    return kr - qr + NA_KH // 2 - 1 if 0 <= kr - qr < NA_KH else None


def _na_build_bias(rpb_ref, bias_scr):
    shape = (GRID_W, LANES)
    qc = lax.broadcasted_iota(jnp.int32, shape, 0)
    lane = lax.broadcasted_iota(jnp.int32, shape, 1)
    kc = lane % GRID_W
    col_start = jnp.clip(qc - NA_KW // 2, 0, GRID_W - NA_KW)
    in_cols = (kc >= col_start) & (kc < col_start + NA_KW)
    left = lane < GRID_W
    neg = jnp.full(shape, NEG_BIG, jnp.float32)

    def toeplitz(hh, ro, lane_off):
        vec = jnp.broadcast_to(rpb_ref[hh, ro:ro + 1, :], shape)
        shift = (lane_off - (NA_KW - 1)) % LANES
        return pltpu.roll(vec, shift, 1, stride=1, stride_axis=0)

    for cls in range(3):
        for hh in range(2):
            for qr in range(NA_QROWS):
                for kp in range(NA_KROWS // 2):
                    ro_a = _na_row_offset(cls, qr, 2 * kp)
                    ro_b = _na_row_offset(cls, qr, 2 * kp + 1)
                    blk_a = neg if ro_a is None else jnp.where(in_cols, toeplitz(hh, ro_a, 0), neg)
                    blk_b = neg if ro_b is None else jnp.where(in_cols, toeplitz(hh, ro_b, GRID_W), neg)
                    bias_scr[cls, hh, qr * GRID_W:(qr + 1) * GRID_W, kp * LANES:(kp + 1) * LANES] = (
                        jnp.where(left, blk_a, blk_b))


def _na_kernel(q_ref, k_ref, v_ref, rpb_ref, o_ref, bias_scr, *, n_tiles):
    b = pl.program_id(1)
    t = pl.program_id(2)

    @pl.when((b == 0) & (t == 0))
    def _():
        _na_build_bias(rpb_ref, bias_scr)

    n_keys = NA_KROWS * GRID_W
    tq = NA_QROWS * GRID_W
    rows_total = n_tiles * NA_QROWS
    lane = lax.broadcasted_iota(jnp.int32, (tq, LANES), 1)
    ones = jnp.ones((n_keys, LANES), jnp.bfloat16)

    for sub in range(NA_TILES_PER_STEP):
        tile = t * NA_TILES_PER_STEP + sub
        ws = jnp.clip(tile * NA_QROWS - NA_KH // 2, 0, rows_total - NA_KROWS)
        start = pl.multiple_of(ws * GRID_W, GRID_W)
        kw = k_ref[pl.ds(start, n_keys), :]
        vw = jnp.concatenate([v_ref[pl.ds(start, n_keys), :], ones], axis=1)
        cls = jnp.where(tile == 0, 0, jnp.where(tile == n_tiles - 1, 2, 1))

        q = q_ref[sub * tq:(sub + 1) * tq, :]
        zero = jnp.zeros_like(q)
        outs = []
        for hh in range(2):
            qm = jnp.where((lane >= hh * HEAD_DIM) & (lane < (hh + 1) * HEAD_DIM), q, zero)
            s = lax.dot_general(qm, kw, (((1,), (1,)), ((), ())), preferred_element_type=jnp.float32)
            s = s + bias_scr[cls, hh]
            p = jnp.exp(s - jnp.max(s, axis=-1, keepdims=True))
            pv = jnp.dot(p.astype(jnp.bfloat16), vw, preferred_element_type=jnp.float32)
            outs.append(pv[:, :LANES] / pv[:, LANES:])
        o_ref[sub * tq:(sub + 1) * tq, :] = jnp.where(lane < HEAD_DIM, outs[0], outs[1]).astype(o_ref.dtype)


def _neighborhood_attention(qkv, rpb_pad, batch, seq):
    n_pairs = qkv.shape[1] // (3 * LANES)
    tq = NA_QROWS * GRID_W
    n_tiles = seq // tq
    n_steps = n_tiles // NA_TILES_PER_STEP
    blk = NA_TILES_PER_STEP * tq
    assert seq // GRID_W >= NA_KROWS and seq % blk == 0
    kern = functools.partial(_na_kernel, n_tiles=n_tiles)
    return pl.pallas_call(
        kern,
        grid=(n_pairs, batch, n_steps),
        in_specs=[
            pl.BlockSpec((blk, LANES), lambda hp, b, t: (b * n_steps + t, hp)),
            pl.BlockSpec((seq, LANES), lambda hp, b, t: (b, n_pairs + hp)),
            pl.BlockSpec((seq, LANES), lambda hp, b, t: (b, 2 * n_pairs + hp)),
            pl.BlockSpec((2, 2 * NA_KH - 1, LANES), lambda hp, b, t: (hp, 0, 0)),
        ],
        out_specs=pl.BlockSpec((blk, LANES), lambda hp, b, t: (b * n_steps + t, hp)),
        out_shape=jax.ShapeDtypeStruct((batch * seq, n_pairs * LANES), jnp.bfloat16),
        scratch_shapes=[pltpu.VMEM((3, 2, tq, NA_KROWS * GRID_W), jnp.float32)],
        compiler_params=_cparams(("arbitrary", "arbitrary", "arbitrary")),
        name="na_attn",
    )(qkv, qkv, qkv, rpb_pad)


def _ffn_kernel(x_ref, xp_ref, xn_ref, wu_ref, wg_ref, bu_ref, bg_ref, cwu_ref, cwg_ref,
                cbu_ref, cbg_ref, wo_ref, bo_ref, g_ref, beta_ref, y_ref, xb_scr, hu_scr, hg_scr,
                acc_scr, *, tiles_per_seq):
    i = pl.program_id(0)
    j = pl.program_id(1)
    tm = x_ref.shape[0]
    halo = FFN_HALO

    @pl.when(j == 0)
    def _():
        xb_scr[:halo, :] = xp_ref[...].astype(jnp.bfloat16)
        xb_scr[halo:halo + tm, :] = x_ref[...].astype(jnp.bfloat16)
        xb_scr[halo + tm:, :] = xn_ref[...].astype(jnp.bfloat16)
        acc_scr[...] = jnp.zeros_like(acc_scr)

    has_prev = (i % tiles_per_seq != 0).astype(jnp.float32)
    has_next = (i % tiles_per_seq != tiles_per_seq - 1).astype(jnp.float32)
    n_sub = tm // FFN_RC

    def up_proj(r, w_ref, b_ref, h_scr):
        lo = 0 if r == 0 else r * FFN_RC + 2 * halo
        hi = (r + 1) * FFN_RC + 2 * halo
        h = jnp.dot(xb_scr[lo:hi, :], w_ref[...], preferred_element_type=jnp.float32) + b_ref[...]
        if r == 0:
            h_scr[:halo, :] = h[:halo] * has_prev
            h_scr[halo:hi, :] = h[halo:]
        elif r == n_sub - 1:
            h_scr[lo:hi - halo, :] = h[:hi - halo - lo]
            h_scr[hi - halo:hi, :] = h[hi - halo - lo:] * has_next
        else:
            h_scr[lo:hi, :] = h

    def conv(r, cw_ref, cb_ref, h_scr):
        cw = cw_ref[...]
        base = r * FFN_RC + halo
        return (h_scr[pl.ds(base - 1, FFN_RC), :] * cw[0:1, :] + h_scr[pl.ds(base, FFN_RC), :] * cw[1:2, :]
                + h_scr[pl.ds(base + 1, FFN_RC), :] * cw[2:3, :] + cb_ref[...])

    up_proj(0, wu_ref, bu_ref, hu_scr)
    up_proj(0, wg_ref, bg_ref, hg_scr)
    for r in range(n_sub):
        if r + 1 < n_sub:
            up_proj(r + 1, wu_ref, bu_ref, hu_scr)
            up_proj(r + 1, wg_ref, bg_ref, hg_scr)
        u = conv(r, cwu_ref, cbu_ref, hu_scr)
        g = conv(r, cwg_ref, cbg_ref, hg_scr)
        gelu = 0.5 * g * (1.0 + lax.erf(g * (1.0 / math.sqrt(2.0))))
        a = (u * gelu).astype(jnp.bfloat16)
        acc_scr[r * FFN_RC:(r + 1) * FFN_RC, :] += jnp.dot(a, wo_ref[...],
                                                         preferred_element_type=jnp.float32)

    @pl.when(j == pl.num_programs(1) - 1)
    def _():
        z = ALPHA * x_ref[...] + acc_scr[...] + bo_ref[...]
        y_ref[...] = _layer_norm_rows(z, g_ref[...], beta_ref[...])


def _conv_ffn_ln(x, seq, w_in, b_in, conv_w, conv_b, w_out, b_out, g, beta):
    m, d = x.shape
    d_ff = w_out.shape[0]
    nj = d_ff // FFN_TN
    n_i = m // FFN_TM
    tiles_per_seq = seq // FFN_TM
    assert seq % FFN_TM == 0 and FFN_TM % FFN_RC == 0 and FFN_TM // FFN_RC >= 2
    halo_per_tile = FFN_TM // FFN_HALO
    n_halo_blocks = m // FFN_HALO
    u_col = lambda i, j: (0, j)
    g_col = lambda i, j: (0, nj + j)
    fixed = lambda i, j: (0, 0)
    kern = functools.partial(_ffn_kernel, tiles_per_seq=tiles_per_seq)
    h_shape = (FFN_TM + 2 * FFN_HALO, FFN_TN)
    return pl.pallas_call(
        kern,
        grid=(n_i, nj),
        in_specs=[
            pl.BlockSpec((FFN_TM, d), lambda i, j: (i, 0)),
            pl.BlockSpec((FFN_HALO, d), lambda i, j: (jnp.maximum(i * halo_per_tile - 1, 0), 0)),
            pl.BlockSpec((FFN_HALO, d),
                         lambda i, j: (jnp.minimum((i + 1) * halo_per_tile, n_halo_blocks - 1), 0)),
            pl.BlockSpec((d, FFN_TN), u_col),
            pl.BlockSpec((d, FFN_TN), g_col),
            pl.BlockSpec((1, FFN_TN), u_col),
            pl.BlockSpec((1, FFN_TN), g_col),
            pl.BlockSpec((3, FFN_TN), u_col),
            pl.BlockSpec((3, FFN_TN), g_col),
            pl.BlockSpec((1, FFN_TN), u_col),
            pl.BlockSpec((1, FFN_TN), g_col),
            pl.BlockSpec((FFN_TN, d), lambda i, j: (j, 0)),
            pl.BlockSpec((1, d), fixed),
            pl.BlockSpec((1, d), fixed),
            pl.BlockSpec((1, d), fixed),
        ],
        out_specs=pl.BlockSpec((FFN_TM, d), lambda i, j: (i, 0)),
        out_shape=jax.ShapeDtypeStruct((m, d), jnp.float32),
        scratch_shapes=[
            pltpu.VMEM((FFN_TM + 2 * FFN_HALO, d), jnp.bfloat16),
            pltpu.VMEM(h_shape, jnp.float32),
            pltpu.VMEM(h_shape, jnp.float32),
            pltpu.VMEM((FFN_TM, d), jnp.float32),
        ],
        compiler_params=_cparams(("parallel", "arbitrary")),
        name="conv_ffn_ln",
    )(x, x, x, w_in, w_in, b_in, b_in, conv_w, conv_w, conv_b, conv_b, w_out, b_out, g, beta)


def kernel(x, attn_w_qkv, attn_w_o, attn_lambda, attn_subln_g, na_w_qkv, na_b_qkv, na_rpb,
           na_w_o, na_b_o, ffn_w_in, ffn_b_in, ffn_conv_w, ffn_conv_b, ffn_w_out, ffn_b_out,
           ln_g, ln_b):
    batch, seq, d = x.shape
    bf16 = jnp.bfloat16
    f32 = jnp.float32
    xf = x.reshape(batch * seq, d)
    q_scale = HEAD_DIM ** -0.5

    n_attn_heads = attn_w_qkv.shape[2] // (3 * 2 * HEAD_DIM)
    slopes = jnp.exp2(-8.0 * jnp.arange(1, n_attn_heads + 1, dtype=f32) / n_attn_heads)
    qk_cols = 2 * n_attn_heads * HEAD_DIM
    attn_scale = jnp.where(jnp.arange(attn_w_qkv.shape[2]) < qk_cols, q_scale, 1.0).astype(f32)[None]
    na_dim = na_w_qkv.shape[2] // 3
    na_scale = jnp.where(jnp.arange(3 * na_dim) < na_dim, q_scale, 1.0).astype(f32)[None]
    zeros_d = jnp.zeros((1, d), f32)
    qf, kf = _alibi_features(n_attn_heads, seq)
    diag = _diag_correction()

    for i in range(DEPTH):
        j = i // 2
        if i % 2 == 0:
            lambda_init = 0.8 - 0.6 * math.exp(-0.3 * i)
            w = attn_w_qkv[j].astype(bf16)
            qkv = _qkv_proj(xf, w, jnp.zeros((1, w.shape[1]), f32), attn_scale)
            o = _diff_attention(qkv, slopes, qf, kf, diag, attn_lambda[j], attn_subln_g[j][None],
                                batch, seq, lambda_init)
            w_o, b_o = attn_w_o[j].astype(bf16), zeros_d
        else:
            qkv = _qkv_proj(xf, na_w_qkv[j].astype(bf16), na_b_qkv[j][None], na_scale)
            rpb_pad = jnp.pad(na_rpb[j], ((0, 0), (0, 0), (0, LANES - na_rpb.shape[3])))
            o = _neighborhood_attention(qkv, rpb_pad, batch, seq)
            w_o, b_o = na_w_o[j].astype(bf16), na_b_o[j][None]
        xf = _out_proj_ln(o, xf, w_o, b_o, ln_g[i, 0][None], ln_b[i, 0][None])
        xf = _conv_ffn_ln(xf, seq, ffn_w_in[i].astype(bf16), ffn_b_in[i][None], ffn_conv_w[i],
                          ffn_conv_b[i][None], ffn_w_out[i].astype(bf16), ffn_b_out[i][None],
                          ln_g[i, 1][None], ln_b[i, 1][None])
    return xf.reshape(batch, seq, d)
```
